```python
import jax, jax.numpy as jnp
from jax import lax
import numpy as np

D_MODEL = 1024
BATCH = 4
SEQ = 8192
DEPTH = 1

ATTN_WIDTH = D_MODEL // 2
HEAD_DIM = 64
N_HEADS = ATTN_WIDTH // HEAD_DIM
CONV_WIDTH_CH = D_MODEL - ATTN_WIDTH
CONV_GROUPS = CONV_WIDTH_CH // HEAD_DIM
CONV_KERNEL = 31
D_FF = 2816
Q_BLOCK = 128
N_SUBLAYERS = 3
MIX_IN = 3 * ATTN_WIDTH + 2 * CONV_WIDTH_CH
RMS_EPS = 1e-6
LN_EPS = 1e-5

kernel_name = "hybrid_stickbreak_conformer_macaron_block"


def rms_norm(x, g, eps=RMS_EPS):
    xf = x.astype(jnp.float32)
    y = xf * lax.rsqrt(jnp.mean(xf * xf, axis=-1, keepdims=True) + eps)
    return (y * g.astype(jnp.float32)).astype(x.dtype)


def layer_norm(x, g, b, eps=LN_EPS):
    xf = x.astype(jnp.float32)
    mu = jnp.mean(xf, axis=-1, keepdims=True)
    var = jnp.mean(jnp.square(xf - mu), axis=-1, keepdims=True)
    y = (xf - mu) * lax.rsqrt(var + eps)
    return (y * g.astype(jnp.float32) + b.astype(jnp.float32)).astype(x.dtype)


def modulate(h, shift, scale):
    return h * (1.0 + scale[:, None, :]) + shift[:, None, :]


def swiglu_ffn(h, w_in, w_out):
    gate, up = jnp.split(h @ w_in, 2, axis=-1)
    return (jax.nn.silu(gate) * up) @ w_out


def stick_breaking_attention(q, k, v):
    seq = q.shape[2]
    scale = HEAD_DIM ** -0.5
    qf = q.astype(jnp.float32) * scale
    kf = k.astype(jnp.float32)
    vf = v.astype(jnp.float32)
    outs = []
    for start in range(0, seq, Q_BLOCK):
        end = start + Q_BLOCK
        q_blk = qf[:, :, start:end]
        k_ctx = kf[:, :, :end]
        v_ctx = vf[:, :, :end]
        z = jnp.einsum('bhqd,bhkd->bhqk', q_blk, k_ctx)
        q_pos = jnp.arange(start, end)[:, None]
        k_pos = jnp.arange(end)[None, :]
        strict = k_pos < q_pos
        log_one_minus = jnp.where(strict, jax.nn.log_sigmoid(-z), 0.0)
        after = lax.cumsum(log_one_minus, axis=3, reverse=True) - log_one_minus
        log_w = jax.nn.log_sigmoid(z) + after
        w = jnp.where(strict, jnp.exp(log_w), 0.0)
        outs.append(jnp.einsum('bhqk,bhkd->bhqd', w, v_ctx))
    return jnp.concatenate(outs, axis=2).astype(q.dtype)


def causal_depthwise_conv(u, w, b):
    y = lax.conv_general_dilated(
        u, w[:, None, :].astype(u.dtype), window_strides=(1,),
        padding=[(CONV_KERNEL - 1, 0)],
        dimension_numbers=('NWC', 'WIO', 'NWC'),
        feature_group_count=u.shape[-1])
    return y + b


def hybrid_mixer(h, w_in_mix, g_attn_out, conv_w, conv_b, conv_ln_g, conv_ln_b, w_out_mix):
    bsz, seq, _ = h.shape
    proj = h @ w_in_mix
    q, k, v, cv, cg = jnp.split(
        proj, [ATTN_WIDTH, 2 * ATTN_WIDTH, 3 * ATTN_WIDTH, 3 * ATTN_WIDTH + CONV_WIDTH_CH], axis=-1)

    def heads(t):
        return t.reshape(bsz, seq, N_HEADS, HEAD_DIM).transpose(0, 2, 1, 3)

    a = stick_breaking_attention(heads(q), heads(k), heads(v))
    a = rms_norm(a, g_attn_out[:, None, :])
    a = a.transpose(0, 2, 1, 3).reshape(bsz, seq, ATTN_WIDTH)

    u = cv * jax.nn.sigmoid(cg)
    u = causal_depthwise_conv(u, conv_w, conv_b)
    u = jax.nn.silu(layer_norm(u, conv_ln_g, conv_ln_b))

    return jnp.concatenate([a, u], axis=-1) @ w_out_mix


def sandwich_sublayer(x, g_pre, g_post, shift, scale, gate, res_w, fn):
    h = modulate(rms_norm(x, g_pre), shift, scale)
    y = rms_norm(fn(h), g_post)
    return x + res_w * (1.0 + gate[:, None, :]) * y


def setup_inputs(seed: int = 0) -> dict:
    key = jax.random.key(seed)
    ks = jax.random.split(key, 24)
    f32 = jnp.float32

    def nrm(k, shape, s):
        return jax.random.normal(k, shape, f32) * s

    def gain(k, n):
        return 1.0 + 0.02 * jax.random.normal(k, (n,), f32)

    return {
        "x": jax.random.normal(ks[0], (BATCH, SEQ, D_MODEL), f32),
        "c": jax.random.normal(ks[1], (BATCH, D_MODEL), f32),
        "w_ada": nrm(ks[2], (D_MODEL, 3 * N_SUBLAYERS * D_MODEL), 0.1 * D_MODEL ** -0.5),
        "b_ada": nrm(ks[3], (3 * N_SUBLAYERS * D_MODEL,), 0.02),
        "g_pre_ff1": gain(ks[4], D_MODEL),
        "g_post_ff1": gain(ks[5], D_MODEL),
        "ff1_w_in": nrm(ks[6], (D_MODEL, 2 * D_FF), D_MODEL ** -0.5),
        "ff1_w_out": nrm(ks[7], (D_FF, D_MODEL), D_FF ** -0.5),
        "g_pre_mix": gain(ks[8], D_MODEL),
        "g_post_mix": gain(ks[9], D_MODEL),
        "w_in_mix": nrm(ks[10], (D_MODEL, MIX_IN), D_MODEL ** -0.5),
        "g_attn_out": 1.0 + 0.02 * jax.random.normal(ks[11], (N_HEADS, HEAD_DIM), f32),
        "conv_w": nrm(ks[12], (CONV_KERNEL, CONV_WIDTH_CH), CONV_KERNEL ** -0.5),
        "conv_b": nrm(ks[13], (CONV_WIDTH_CH,), 0.02),
        "conv_ln_g": gain(ks[14], CONV_WIDTH_CH),
        "conv_ln_b": nrm(ks[15], (CONV_WIDTH_CH,), 0.02),
        "w_out_mix": nrm(ks[16], (D_MODEL, D_MODEL), D_MODEL ** -0.5),
        "g_pre_ff2": gain(ks[17], D_MODEL),
        "g_post_ff2": gain(ks[18], D_MODEL),
        "ff2_w_in": nrm(ks[19], (D_MODEL, 2 * D_FF), D_MODEL ** -0.5),
        "ff2_w_out": nrm(ks[20], (D_FF, D_MODEL), D_FF ** -0.5),
    }


def reference(x, c, w_ada, b_ada, g_pre_ff1, g_post_ff1, ff1_w_in, ff1_w_out,
              g_pre_mix, g_post_mix, w_in_mix, g_attn_out, conv_w, conv_b,
              conv_ln_g, conv_ln_b, w_out_mix, g_pre_ff2, g_post_ff2,
              ff2_w_in, ff2_w_out):
    mod = (jax.nn.silu(c) @ w_ada + b_ada).reshape(c.shape[0], N_SUBLAYERS, 3, D_MODEL)
    h = x
    for _layer in range(DEPTH):
        h = sandwich_sublayer(
            h, g_pre_ff1, g_post_ff1, mod[:, 0, 0], mod[:, 0, 1], mod[:, 0, 2], 0.5,
            lambda t: swiglu_ffn(t, ff1_w_in, ff1_w_out))
        h = sandwich_sublayer(
            h, g_pre_mix, g_post_mix, mod[:, 1, 0], mod[:, 1, 1], mod[:, 1, 2], 1.0,
            lambda t: hybrid_mixer(t, w_in_mix, g_attn_out, conv_w, conv_b,
                                   conv_ln_g, conv_ln_b, w_out_mix))
        h = sandwich_sublayer(
            h, g_pre_ff2, g_post_ff2, mod[:, 2, 0], mod[:, 2, 1], mod[:, 2, 2], 0.5,
            lambda t: swiglu_ffn(t, ff2_w_in, ff2_w_out))
    return h
```

```python
import functools

import jax
import jax.numpy as jnp
from jax import lax
from jax.experimental import pallas as pl
from jax.experimental.pallas import tpu as pltpu

D_MODEL = 1024
ATTN_WIDTH = D_MODEL // 2
HEAD_DIM = 64
N_HEADS = ATTN_WIDTH // HEAD_DIM
CONV_WIDTH_CH = D_MODEL - ATTN_WIDTH
CONV_KERNEL = 31
D_FF = 2816
N_SUBLAYERS = 3
MIX_IN = 3 * ATTN_WIDTH + 2 * CONV_WIDTH_CH
RMS_EPS = 1e-6
LN_EPS = 1e-5

LANES = 128
SUBLANES = 8
MXU_DIM = 256
VMEM_LIMIT = 56 * 1024 * 1024

FFN_TM = 512
FFN_CHUNK = D_FF // 2
MIX_TM = 512
OUT_TM = 256
CONV_ROWS = 32
HALO = 32
ATT_Q = 128
ATT_K0 = 256
ATT_KB = 128
LOGW_FLOOR = -88.0

F32 = jnp.float32
BF16 = jnp.bfloat16


def _sigmoid(x):
    return 1.0 / (1.0 + jnp.exp(-x))


def _dot(a, b):
    return jnp.dot(a, b, preferred_element_type=F32)


def _rms(x, g):
    ms = jnp.mean(x * x, axis=-1, keepdims=True)
    return x * lax.rsqrt(ms + RMS_EPS) * g


def _adaln_kernel(c_ref, w_ref, b_ref, o_ref):
    c = c_ref[...]
    s = c * _sigmoid(c)
    o_ref[...] = _dot(s.astype(BF16), w_ref[...].astype(BF16)) + b_ref[...]


def _adaln(c, w_ada, b_ada):
    bsz = c.shape[0]
    n = w_ada.shape[1]
    tn = 1024
    c_pad = jnp.zeros((SUBLANES, D_MODEL), F32).at[:bsz].set(c)
    out = pl.pallas_call(
        _adaln_kernel,
        grid=(n // tn,),
        in_specs=[
            pl.BlockSpec((SUBLANES, D_MODEL), lambda j: (0, 0)),
            pl.BlockSpec((D_MODEL, tn), lambda j: (0, j)),
            pl.BlockSpec((1, tn), lambda j: (0, j)),
        ],
        out_specs=pl.BlockSpec((SUBLANES, tn), lambda j: (0, j)),
        out_shape=jax.ShapeDtypeStruct((SUBLANES, n), F32),
        name="adaln",
    )(c_pad, w_ada, b_ada.reshape(1, n))
    return out[:bsz].reshape(bsz, 3 * N_SUBLAYERS, D_MODEL)


def _modulated(x, mod_ref, sub, g_pre):
    shift = mod_ref[0, 3 * sub:3 * sub + 1, :]
    scale = mod_ref[0, 3 * sub + 1:3 * sub + 2, :]
    return _rms(x, g_pre) * (1.0 + scale) + shift


def _ffn_kernel(x_ref, mod_ref, gpre_ref, gpost_ref, win_ref, wout_ref, o_ref,
                *, sub, res_w):
    x = x_ref[0]
    hb = _modulated(x, mod_ref, sub, gpre_ref[...]).astype(BF16)
    f = None
    for c in range(D_FF // FFN_CHUNK):
        lo = c * FFN_CHUNK
        g = _dot(hb, win_ref[:, lo:lo + FFN_CHUNK])
        u = _dot(hb, win_ref[:, D_FF + lo:D_FF + lo + FFN_CHUNK])
        act = (g * _sigmoid(g) * u).astype(BF16)
        part = _dot(act, wout_ref[lo:lo + FFN_CHUNK, :])
        f = part if f is None else f + part
    y = _rms(f, gpost_ref[...])
    gate = mod_ref[0, 3 * sub + 2:3 * sub + 3, :]
    o_ref[0] = x + res_w * (1.0 + gate) * y


def _const_spec(shape):
    nd = len(shape)
    return pl.BlockSpec(shape, lambda *_: (0,) * nd, pipeline_mode=pl.Buffered(1))


def _ffn(x, mod, g_pre, g_post, w_in, w_out, sub, res_w):
    bsz, seq, d = x.shape
    tm = FFN_TM
    return pl.pallas_call(
        functools.partial(_ffn_kernel, sub=sub, res_w=res_w),
        grid=(bsz, seq // tm),
        in_specs=[
            pl.BlockSpec((1, tm, d), lambda b, i: (b, i, 0)),
            pl.BlockSpec((1, 3 * N_SUBLAYERS, d), lambda b, i: (b, 0, 0)),
            _const_spec((1, d)),
            _const_spec((1, d)),
            _const_spec((d, 2 * D_FF)),
            _const_spec((D_FF, d)),
        ],
        out_specs=pl.BlockSpec((1, tm, d), lambda b, i: (b, i, 0)),
        out_shape=jax.ShapeDtypeStruct(x.shape, F32),
        compiler_params=pltpu.CompilerParams(vmem_limit_bytes=VMEM_LIMIT),
        name=f"ffn{sub}",
    )(x, mod, g_pre.reshape(1, d), g_post.reshape(1, d), w_in, w_out)


def _mixin_kernel(x_ref, mod_ref, gpre_ref, w_ref, qkv_ref, u_ref):
    hb = _modulated(x_ref[0], mod_ref, 1, gpre_ref[...]).astype(BF16)
    aw = ATTN_WIDTH
    q = _dot(hb, w_ref[:, 0:aw]) * (HEAD_DIM ** -0.5)
    qkv_ref[0, :, 0:aw] = q.astype(BF16)
    kv = _dot(hb, w_ref[:, aw:3 * aw])
    qkv_ref[0, :, aw:3 * aw] = kv.astype(BF16)
    cv = _dot(hb, w_ref[:, 3 * aw:3 * aw + CONV_WIDTH_CH])
    cg = _dot(hb, w_ref[:, 3 * aw + CONV_WIDTH_CH:MIX_IN])
    u_ref[0] = cv * _sigmoid(cg)


def _mix_in(x, mod, g_pre, w_in_mix):
    bsz, seq, d = x.shape
    tm = MIX_TM
    return pl.pallas_call(
        _mixin_kernel,
        grid=(bsz, seq // tm),
        in_specs=[
            pl.BlockSpec((1, tm, d), lambda b, i: (b, i, 0)),
            pl.BlockSpec((1, 3 * N_SUBLAYERS, d), lambda b, i: (b, 0, 0)),
            _const_spec((1, d)),
            _const_spec((d, MIX_IN)),
        ],
        out_specs=[
            pl.BlockSpec((1, tm, 3 * ATTN_WIDTH), lambda b, i: (b, i, 0)),
            pl.BlockSpec((1, tm, CONV_WIDTH_CH), lambda b, i: (b, i, 0)),
        ],
        out_shape=[
            jax.ShapeDtypeStruct((bsz, seq, 3 * ATTN_WIDTH), BF16),
            jax.ShapeDtypeStruct((bsz, seq, CONV_WIDTH_CH), F32),
        ],
        compiler_params=pltpu.CompilerParams(vmem_limit_bytes=VMEM_LIMIT),
        name="mix_in",
    )(x, mod, g_pre.reshape(1, d), w_in_mix)


def _log_one_minus_beta(z):
    return -(jnp.maximum(z, 0.0) + jnp.log(1.0 + jnp.exp(-jnp.abs(z))))


def _suffix_sums(lom, tri):
    hi = lom.astype(BF16)
    lo = (lom - hi.astype(F32)).astype(BF16)
    return _dot(hi, tri) + _dot(lo, tri)


def _qk(qm, k):
    return lax.dot_general(qm, k, (((1,), (1,)), ((), ())),
                           preferred_element_type=F32)


def _attn_kernel(q_ref, k_ref, v_ref, g_ref, o_ref, tri_ref):
    b = pl.program_id(0)
    hp = pl.program_id(1)
    i = pl.program_id(2)

    @pl.when((b == 0) & (hp == 0) & (i == 0))
    def _():
        r = lax.broadcasted_iota(jnp.int32, (ATT_K0, ATT_K0), 0)
        c = lax.broadcasted_iota(jnp.int32, (ATT_K0, ATT_K0), 1)
        tri_ref[...] = jnp.where(r > c, 1.0, 0.0).astype(BF16)

    q = q_ref[0]
    lane = lax.broadcasted_iota(jnp.int32, (1, LANES), 1)
    start0 = pl.multiple_of(jnp.maximum(i - 1, 0) * ATT_Q, ATT_Q)
    k0 = k_ref[0, pl.ds(start0, ATT_K0), :]
    v0 = v_ref[0, pl.ds(start0, ATT_K0), :]
    row = lax.broadcasted_iota(jnp.int32, (ATT_Q, ATT_K0), 0)
    col = lax.broadcasted_iota(jnp.int32, (ATT_Q, ATT_K0), 1)
    strict = (start0 + col) < (i * ATT_Q + row)
    tri = tri_ref[...]
    tri_b = tri_ref[0:ATT_KB, 0:ATT_KB]

    heads = []
    for hh in range(2):
        in_head = (lane < HEAD_DIM) if hh == 0 else (lane >= HEAD_DIM)
        qm = jnp.where(in_head, q, jnp.zeros_like(q))

        z = _qk(qm, k0)
        lom = _log_one_minus_beta(z)
        lomm = jnp.where(strict, lom, 0.0)
        after = _suffix_sums(lomm, tri)
        w = jnp.where(strict, jnp.exp(z + lom + after), 0.0)
        acc = _dot(w.astype(BF16), v0)
        carry = jnp.sum(lomm, axis=-1, keepdims=True)

        def cond(st):
            jb, carry, _ = st
            return jnp.logical_and(jb >= 0, jnp.max(carry) > LOGW_FLOOR)

        def body(st, qm=qm):
            jb, carry, acc = st
            ks = pl.multiple_of(jb * ATT_KB, ATT_KB)
            kb = k_ref[0, pl.ds(ks, ATT_KB), :]
            vb = v_ref[0, pl.ds(ks, ATT_KB), :]
            z = _qk(qm, kb)
            lom = _log_one_minus_beta(z)
            after = _suffix_sums(lom, tri_b)
            w = jnp.exp(z + lom + after + carry)
            acc = acc + _dot(w.astype(BF16), vb)
            carry = carry + jnp.sum(lom, axis=-1, keepdims=True)
            return jb - 1, carry, acc

        _, _, acc = lax.while_loop(cond, body, (i - 2, carry, acc))
        heads.append(acc)

    first = lane < HEAD_DIM
    o = jnp.where(first, heads[0], heads[1])
    sq = o * o
    s_all = jnp.sum(sq, axis=-1, keepdims=True)
    s_first = jnp.sum(jnp.where(first, sq, 0.0), axis=-1, keepdims=True)
    ms = jnp.where(first, s_first, s_all - s_first) * (1.0 / HEAD_DIM)
    o_ref[0] = (o * lax.rsqrt(ms + RMS_EPS) * g_ref[...]).astype(BF16)


def _attention(qkv, g_attn_out):
    bsz, seq, _ = qkv.shape
    n_pair = ATTN_WIDTH // LANES
    return pl.pallas_call(
        _attn_kernel,
        grid=(bsz, n_pair, seq // ATT_Q),
        in_specs=[
            pl.BlockSpec((1, ATT_Q, LANES), lambda b, h, i: (b, i, h)),
            pl.BlockSpec((1, seq, LANES), lambda b, h, i: (b, 0, n_pair + h)),
            pl.BlockSpec((1, seq, LANES), lambda b, h, i: (b, 0, 2 * n_pair + h)),
            pl.BlockSpec((1, LANES), lambda b, h, i: (0, h)),
        ],
        out_specs=pl.BlockSpec((1, ATT_Q, LANES), lambda b, h, i: (b, i, h)),
        out_shape=jax.ShapeDtypeStruct((bsz, seq, ATTN_WIDTH), BF16),
        scratch_shapes=[pltpu.VMEM((ATT_K0, ATT_K0), BF16)],
        compiler_params=pltpu.CompilerParams(
            dimension_semantics=("arbitrary", "arbitrary", "arbitrary"),
            vmem_limit_bytes=VMEM_LIMIT),
        name="attn",
    )(qkv, qkv, qkv, g_attn_out.reshape(1, ATTN_WIDTH))


def _mixout_kernel(x_ref, mod_ref, a_ref, u_ref, halo_ref, cw_ref, cb_ref,
                   lng_ref, lnb_ref, w_ref, gpost_ref, o_ref, buf_ref, act_ref):
    i = pl.program_id(1)
    tm = OUT_TM
    has_prev = (i > 0).astype(F32)
    buf_ref[0:HALO, :] = halo_ref[0] * has_prev
    buf_ref[HALO:HALO + tm, :] = u_ref[0]

    lead = HALO - (CONV_KERNEL - 1)
    for r in range(tm // CONV_ROWS):
        base = r * CONV_ROWS + lead
        y = cb_ref[...] + cw_ref[0:1, :] * buf_ref[base:base + CONV_ROWS, :]
        for t in range(1, CONV_KERNEL):
            y = y + cw_ref[t:t + 1, :] * buf_ref[base + t:base + t + CONV_ROWS, :]
        mu = jnp.mean(y, axis=-1, keepdims=True)
        yc = y - mu
        var = jnp.mean(yc * yc, axis=-1, keepdims=True)
        yn = yc * lax.rsqrt(var + LN_EPS) * lng_ref[...] + lnb_ref[...]
        act_ref[r * CONV_ROWS:(r + 1) * CONV_ROWS, :] = (yn * _sigmoid(yn)).astype(BF16)

    m = _dot(a_ref[0], w_ref[0:ATTN_WIDTH, :]) + _dot(act_ref[...], w_ref[ATTN_WIDTH:D_MODEL, :])
    y = _rms(m, gpost_ref[...])
    gate = mod_ref[0, 5:6, :]
    o_ref[0] = x_ref[0] + (1.0 + gate) * y


def _mix_out(x, mod, a, u, conv_w, conv_b, ln_g, ln_b, w_out_mix, g_post):
    bsz, seq, d = x.shape
    tm = OUT_TM
    cw = CONV_WIDTH_CH
    per = tm // HALO
    return pl.pallas_call(
        _mixout_kernel,
        grid=(bsz, seq // tm),
        in_specs=[
            pl.BlockSpec((1, tm, d), lambda b, i: (b, i, 0)),
            pl.BlockSpec((1, 3 * N_SUBLAYERS, d), lambda b, i: (b, 0, 0)),
            pl.BlockSpec((1, tm, ATTN_WIDTH), lambda b, i: (b, i, 0)),
            pl.BlockSpec((1, tm, cw), lambda b, i: (b, i, 0)),
            pl.BlockSpec((1, HALO, cw), lambda b, i: (b, jnp.maximum(i * per - 1, 0), 0)),
            _const_spec((CONV_KERNEL, cw)),
            _const_spec((1, cw)),
            _const_spec((1, cw)),
            _const_spec((1, cw)),
            _const_spec((d, d)),
            _const_spec((1, d)),
        ],
        out_specs=pl.BlockSpec((1, tm, d), lambda b, i: (b, i, 0)),
        out_shape=jax.ShapeDtypeStruct(x.shape, F32),
        scratch_shapes=[pltpu.VMEM((HALO + tm, cw), F32), pltpu.VMEM((tm, cw), BF16)],
        compiler_params=pltpu.CompilerParams(vmem_limit_bytes=VMEM_LIMIT),
        name="mix_out",
    )(x, mod, a, u, u, conv_w, conv_b.reshape(1, cw), ln_g.reshape(1, cw),
      ln_b.reshape(1, cw), w_out_mix, g_post.reshape(1, d))


def kernel(x, c, w_ada, b_ada, g_pre_ff1, g_post_ff1, ff1_w_in, ff1_w_out, g_pre_mix, g_post_mix, w_in_mix, g_attn_out, conv_w, conv_b, conv_ln_g, conv_ln_b, w_out_mix, g_pre_ff2, g_post_ff2, ff2_w_in, ff2_w_out):
    mod = _adaln(c, w_ada, b_ada)
    h = _ffn(x, mod, g_pre_ff1, g_post_ff1, ff1_w_in.astype(BF16),
             ff1_w_out.astype(BF16), 0, 0.5)
    qkv, u = _mix_in(h, mod, g_pre_mix, w_in_mix.astype(BF16))
    a = _attention(qkv, g_attn_out)
    h = _mix_out(h, mod, a, u, conv_w, conv_b, conv_ln_g, conv_ln_b,
                 w_out_mix.astype(BF16), g_post_mix)
    h = _ffn(h, mod, g_pre_ff2, g_post_ff2, ff2_w_in.astype(BF16),
             ff2_w_out.astype(BF16), 2, 0.5)
    return h
```

```python
import functools

import jax
import jax.numpy as jnp
from jax import lax
from jax.experimental import pallas as pl
from jax.experimental.pallas import tpu as pltpu

D_MODEL = 1024
ATTN_WIDTH = D_MODEL // 2
HEAD_DIM = 64
N_HEADS = ATTN_WIDTH // HEAD_DIM
CONV_WIDTH_CH = D_MODEL - ATTN_WIDTH
CONV_KERNEL = 31
D_FF = 2816
N_SUBLAYERS = 3
MIX_IN = 3 * ATTN_WIDTH + 2 * CONV_WIDTH_CH
RMS_EPS = 1e-6
LN_EPS = 1e-5

LANES = 128
SUBLANES = 8
MXU_DIM = 256
VMEM_LIMIT = 56 * 1024 * 1024

FFN_TM = 512
FFN_CHUNK = D_FF // 2
MIX_TM = 512
OUT_TM = 256
CONV_ROWS = 32
HALO = 32
SHIFT_SPAN = HALO - SUBLANES
ATT_Q = 128
ATT_PAIRS = 4
ATT_K0 = 256
ATT_KB = 128
LOGW_FLOOR = -88.0

F32 = jnp.float32
BF16 = jnp.bfloat16


def _sigmoid(x):
    return 1.0 / (1.0 + jnp.exp(-x))


def _dot(a, b):
    return jnp.dot(a, b, preferred_element_type=F32)


def _rms(x, g):
    ms = jnp.mean(x * x, axis=-1, keepdims=True)
    return x * lax.rsqrt(ms + RMS_EPS) * g


def _adaln_kernel(c_ref, w_ref, b_ref, o_ref):
    c = c_ref[...]
    s = c * _sigmoid(c)
    o_ref[...] = _dot(s.astype(BF16), w_ref[...].astype(BF16)) + b_ref[...]


def _adaln(c, w_ada, b_ada):
    bsz = c.shape[0]
    n = w_ada.shape[1]
    tn = 1024
    c_pad = jnp.zeros((SUBLANES, D_MODEL), F32).at[:bsz].set(c)
    out = pl.pallas_call(
        _adaln_kernel,
        grid=(n // tn,),
        in_specs=[
            pl.BlockSpec((SUBLANES, D_MODEL), lambda j: (0, 0)),
            pl.BlockSpec((D_MODEL, tn), lambda j: (0, j)),
            pl.BlockSpec((1, tn), lambda j: (0, j)),
        ],
        out_specs=pl.BlockSpec((SUBLANES, tn), lambda j: (0, j)),
        out_shape=jax.ShapeDtypeStruct((SUBLANES, n), F32),
        name="adaln",
    )(c_pad, w_ada, b_ada.reshape(1, n))
    return out[:bsz].reshape(bsz, 3 * N_SUBLAYERS, D_MODEL)


def _modulated(x, mod_ref, sub, g_pre):
    shift = mod_ref[0, 3 * sub:3 * sub + 1, :]
    scale = mod_ref[0, 3 * sub + 1:3 * sub + 2, :]
    return _rms(x, g_pre) * (1.0 + scale) + shift


def _ffn_kernel(x_ref, mod_ref, gpre_ref, gpost_ref, win_ref, wout_ref, o_ref,
                *, sub, res_w):
    x = x_ref[0]
    hb = _modulated(x, mod_ref, sub, gpre_ref[...]).astype(BF16)
    f = None
    for c in range(D_FF // FFN_CHUNK):
        lo = c * FFN_CHUNK
        g = _dot(hb, win_ref[:, lo:lo + FFN_CHUNK])
        u = _dot(hb, win_ref[:, D_FF + lo:D_FF + lo + FFN_CHUNK])
        act = (g * _sigmoid(g) * u).astype(BF16)
        part = _dot(act, wout_ref[lo:lo + FFN_CHUNK, :])
        f = part if f is None else f + part
    y = _rms(f, gpost_ref[...])
    gate = mod_ref[0, 3 * sub + 2:3 * sub + 3, :]
    o_ref[0] = x + res_w * (1.0 + gate) * y


def _const_spec(shape):
    nd = len(shape)
    return pl.BlockSpec(shape, lambda *_: (0,) * nd, pipeline_mode=pl.Buffered(1))


def _ffn(x, mod, g_pre, g_post, w_in, w_out, sub, res_w):
    bsz, seq, d = x.shape
    tm = FFN_TM
    return pl.pallas_call(
        functools.partial(_ffn_kernel, sub=sub, res_w=res_w),
        grid=(bsz, seq // tm),
        in_specs=[
            pl.BlockSpec((1, tm, d), lambda b, i: (b, i, 0)),
            pl.BlockSpec((1, 3 * N_SUBLAYERS, d), lambda b, i: (b, 0, 0)),
            _const_spec((1, d)),
            _const_spec((1, d)),
            _const_spec((d, 2 * D_FF)),
            _const_spec((D_FF, d)),
        ],
        out_specs=pl.BlockSpec((1, tm, d), lambda b, i: (b, i, 0)),
        out_shape=jax.ShapeDtypeStruct(x.shape, F32),
        compiler_params=pltpu.CompilerParams(vmem_limit_bytes=VMEM_LIMIT),
        name=f"ffn{sub}",
    )(x, mod, g_pre.reshape(1, d), g_post.reshape(1, d), w_in, w_out)


def _mixin_kernel(x_ref, mod_ref, gpre_ref, w_ref, qkv_ref, u_ref):
    hb = _modulated(x_ref[0], mod_ref, 1, gpre_ref[...]).astype(BF16)
    aw = ATTN_WIDTH
    q = _dot(hb, w_ref[:, 0:aw]) * (HEAD_DIM ** -0.5)
    qkv_ref[0, :, 0:aw] = q.astype(BF16)
    kv = _dot(hb, w_ref[:, aw:3 * aw])
    qkv_ref[0, :, aw:3 * aw] = kv.astype(BF16)
    cv = _dot(hb, w_ref[:, 3 * aw:3 * aw + CONV_WIDTH_CH])
    cg = _dot(hb, w_ref[:, 3 * aw + CONV_WIDTH_CH:MIX_IN])
    u_ref[0] = cv * _sigmoid(cg)


def _mix_in(x, mod, g_pre, w_in_mix):
    bsz, seq, d = x.shape
    tm = MIX_TM
    return pl.pallas_call(
        _mixin_kernel,
        grid=(bsz, seq // tm),
        in_specs=[
            pl.BlockSpec((1, tm, d), lambda b, i: (b, i, 0)),
            pl.BlockSpec((1, 3 * N_SUBLAYERS, d), lambda b, i: (b, 0, 0)),
            _const_spec((1, d)),
            _const_spec((d, MIX_IN)),
        ],
        out_specs=[
            pl.BlockSpec((1, tm, 3 * ATTN_WIDTH), lambda b, i: (b, i, 0)),
            pl.BlockSpec((1, tm, CONV_WIDTH_CH), lambda b, i: (b, i, 0)),
        ],
        out_shape=[
            jax.ShapeDtypeStruct((bsz, seq, 3 * ATTN_WIDTH), BF16),
            jax.ShapeDtypeStruct((bsz, seq, CONV_WIDTH_CH), F32),
        ],
        compiler_params=pltpu.CompilerParams(vmem_limit_bytes=VMEM_LIMIT),
        name="mix_in",
    )(x, mod, g_pre.reshape(1, d), w_in_mix)


def _log_one_minus_beta(z):
    return -(jnp.maximum(z, 0.0) + jnp.log(1.0 + jnp.exp(-jnp.abs(z))))


def _suffix_sums(lom, tri):
    hi = lom.astype(BF16)
    lo = (lom - hi.astype(F32)).astype(BF16)
    return _dot(hi, tri) + _dot(lo, tri)


def _qk(qm, k):
    return lax.dot_general(qm, k, (((1,), (1,)), ((), ())),
                           preferred_element_type=F32)


def _attn_kernel(q_ref, k_ref, v_ref, g_ref, o_ref, tri_ref):
    b = pl.program_id(0)
    grp = pl.program_id(1)
    i = pl.program_id(2)

    @pl.when((b == 0) & (grp == 0) & (i == 0))
    def _():
        r = lax.broadcasted_iota(jnp.int32, (ATT_K0, ATT_K0), 0)
        c = lax.broadcasted_iota(jnp.int32, (ATT_K0, ATT_K0), 1)
        tri_ref[...] = jnp.where(r >= c, 1.0, 0.0).astype(BF16)

    first = lax.broadcasted_iota(jnp.int32, (1, LANES), 1) < HEAD_DIM
    start0 = pl.multiple_of(jnp.maximum(i - 1, 0) * ATT_Q, ATT_Q)
    row = lax.broadcasted_iota(jnp.int32, (2 * ATT_Q, ATT_K0), 0)
    col = lax.broadcasted_iota(jnp.int32, (2 * ATT_Q, ATT_K0), 1)
    q_pos = i * ATT_Q + jnp.bitwise_and(row, ATT_Q - 1)
    strict = (start0 + col) < q_pos
    tri = tri_ref[...]
    tri_b = tri_ref[0:ATT_KB, 0:ATT_KB]

    def stacked_q(p):
        qp = q_ref[0, :, p * LANES:(p + 1) * LANES]
        zero = jnp.zeros_like(qp)
        return jnp.concatenate(
            [jnp.where(first, qp, zero), jnp.where(first, zero, qp)], axis=0)

    def weighted_values(w, v):
        wb = w.astype(BF16)
        return jnp.where(first, _dot(wb[0:ATT_Q], v), _dot(wb[ATT_Q:2 * ATT_Q], v))

    qs, carries, accs = [], [], []
    for p in range(ATT_PAIRS):
        lanes = slice(p * LANES, (p + 1) * LANES)
        k0 = k_ref[0, pl.ds(start0, ATT_K0), lanes]
        v0 = v_ref[0, pl.ds(start0, ATT_K0), lanes]
        q2 = stacked_q(p)
        z = _qk(q2, k0)
        lom = _log_one_minus_beta(z)
        lomm = jnp.where(strict, lom, 0.0)
        w = jnp.where(strict, jnp.exp(z + _suffix_sums(lomm, tri)), 0.0)
        qs.append(q2)
        accs.append(weighted_values(w, v0))
        carries.append(jnp.sum(lomm, axis=-1, keepdims=True))

    def cond(st):
        jb, carries, _ = st
        worst = functools.reduce(jnp.maximum, carries)
        return jnp.logical_and(jb >= 0, jnp.max(worst) > LOGW_FLOOR)

    def body(st):
        jb, carries, accs = st
        ks = pl.multiple_of(jb * ATT_KB, ATT_KB)
        new_c, new_a = [], []
        for p in range(ATT_PAIRS):
            lanes = slice(p * LANES, (p + 1) * LANES)
            kb = k_ref[0, pl.ds(ks, ATT_KB), lanes]
            vb = v_ref[0, pl.ds(ks, ATT_KB), lanes]
            z = _qk(qs[p], kb)
            lom = _log_one_minus_beta(z)
            w = jnp.exp(z + _suffix_sums(lom, tri_b) + carries[p])
            new_a.append(accs[p] + weighted_values(w, vb))
            new_c.append(carries[p] + jnp.sum(lom, axis=-1, keepdims=True))
        return jb - 1, tuple(new_c), tuple(new_a)

    _, _, accs = lax.while_loop(cond, body, (i - 2, tuple(carries), tuple(accs)))

    for p in range(ATT_PAIRS):
        lanes = slice(p * LANES, (p + 1) * LANES)
        o = accs[p]
        sq = o * o
        s_all = jnp.sum(sq, axis=-1, keepdims=True)
        s_first = jnp.sum(jnp.where(first, sq, 0.0), axis=-1, keepdims=True)
        ms = jnp.where(first, s_first, s_all - s_first) * (1.0 / HEAD_DIM)
        o_ref[0, :, lanes] = (o * lax.rsqrt(ms + RMS_EPS) * g_ref[:, lanes]).astype(BF16)


def _attention(qkv, g_attn_out):
    bsz, seq, _ = qkv.shape
    width = ATT_PAIRS * LANES
    n_grp = ATTN_WIDTH // width
    return pl.pallas_call(
        _attn_kernel,
        grid=(bsz, n_grp, seq // ATT_Q),
        in_specs=[
            pl.BlockSpec((1, ATT_Q, width), lambda b, g, i: (b, i, g)),
            pl.BlockSpec((1, seq, width), lambda b, g, i: (b, 0, n_grp + g)),
            pl.BlockSpec((1, seq, width), lambda b, g, i: (b, 0, 2 * n_grp + g)),
            pl.BlockSpec((1, width), lambda b, g, i: (0, g)),
        ],
        out_specs=pl.BlockSpec((1, ATT_Q, width), lambda b, g, i: (b, i, g)),
        out_shape=jax.ShapeDtypeStruct((bsz, seq, ATTN_WIDTH), BF16),
        scratch_shapes=[pltpu.VMEM((ATT_K0, ATT_K0), BF16)],
        compiler_params=pltpu.CompilerParams(
            dimension_semantics=("arbitrary", "arbitrary", "arbitrary"),
            vmem_limit_bytes=VMEM_LIMIT),
        name="attn",
    )(qkv, qkv, qkv, g_attn_out.reshape(1, ATTN_WIDTH))


def _mixout_kernel(x_ref, mod_ref, a_ref, u_ref, halo_ref, cw_ref, cb_ref,
                   lng_ref, lnb_ref, w_ref, gpost_ref, o_ref, buf_ref, shift_ref,
                   act_ref):
    i = pl.program_id(1)
    tm = OUT_TM
    has_prev = (i > 0).astype(F32)
    buf_ref[0:HALO, :] = halo_ref[0] * has_prev
    buf_ref[HALO:HALO + tm, :] = u_ref[0]

    span = tm + SHIFT_SPAN
    for s in range(1, SUBLANES):
        shift_ref[s - 1] = buf_ref[s:s + span, :]

    lead = HALO - (CONV_KERNEL - 1)
    for r in range(tm // CONV_ROWS):
        groups = CONV_ROWS // SUBLANES
        y = jnp.broadcast_to(cb_ref[...][None], (groups, SUBLANES, CONV_WIDTH_CH))
        for t in range(CONV_KERNEL):
            s = (lead + t) % SUBLANES
            base = r * CONV_ROWS + lead + t - s
            if s == 0:
                rows = buf_ref[base:base + CONV_ROWS, :]
            else:
                rows = shift_ref[s - 1, base:base + CONV_ROWS, :]
            y = y + cw_ref[t][None] * rows.reshape(groups, SUBLANES, CONV_WIDTH_CH)
        y = y.reshape(CONV_ROWS, CONV_WIDTH_CH)
        mu = jnp.mean(y, axis=-1, keepdims=True)
        yc = y - mu
        var = jnp.mean(yc * yc, axis=-1, keepdims=True)
        yn = yc * lax.rsqrt(var + LN_EPS) * lng_ref[...] + lnb_ref[...]
        act_ref[r * CONV_ROWS:(r + 1) * CONV_ROWS, :] = (yn * _sigmoid(yn)).astype(BF16)

    m = _dot(a_ref[0], w_ref[0:ATTN_WIDTH, :]) + _dot(act_ref[...], w_ref[ATTN_WIDTH:D_MODEL, :])
    y = _rms(m, gpost_ref[...])
    gate = mod_ref[0, 5:6, :]
    o_ref[0] = x_ref[0] + (1.0 + gate) * y


def _mix_out(x, mod, a, u, conv_w, conv_b, ln_g, ln_b, w_out_mix, g_post):
    bsz, seq, d = x.shape
    tm = OUT_TM
    cw = CONV_WIDTH_CH
    per = tm // HALO
    return pl.pallas_call(
        _mixout_kernel,
        grid=(bsz, seq // tm),
        in_specs=[
            pl.BlockSpec((1, tm, d), lambda b, i: (b, i, 0)),
            pl.BlockSpec((1, 3 * N_SUBLAYERS, d), lambda b, i: (b, 0, 0)),
            pl.BlockSpec((1, tm, ATTN_WIDTH), lambda b, i: (b, i, 0)),
            pl.BlockSpec((1, tm, cw), lambda b, i: (b, i, 0)),
            pl.BlockSpec((1, HALO, cw), lambda b, i: (b, jnp.maximum(i * per - 1, 0), 0)),
            _const_spec((CONV_KERNEL, SUBLANES, cw)),
            _const_spec((SUBLANES, cw)),
            _const_spec((1, cw)),
            _const_spec((1, cw)),
            _const_spec((d, d)),
            _const_spec((1, d)),
        ],
        out_specs=pl.BlockSpec((1, tm, d), lambda b, i: (b, i, 0)),
        out_shape=jax.ShapeDtypeStruct(x.shape, F32),
        scratch_shapes=[pltpu.VMEM((HALO + tm, cw), F32),
                        pltpu.VMEM((SUBLANES - 1, tm + SHIFT_SPAN, cw), F32),
                        pltpu.VMEM((tm, cw), BF16)],
        compiler_params=pltpu.CompilerParams(vmem_limit_bytes=VMEM_LIMIT),
        name="mix_out",
    )(x, mod, a, u, u,
      jnp.broadcast_to(conv_w[:, None, :], (CONV_KERNEL, SUBLANES, cw)),
      jnp.broadcast_to(conv_b[None, :], (SUBLANES, cw)), ln_g.reshape(1, cw),
      ln_b.reshape(1, cw), w_out_mix, g_post.reshape(1, d))


def kernel(x, c, w_ada, b_ada, g_pre_ff1, g_post_ff1, ff1_w_in, ff1_w_out, g_pre_mix, g_post_mix, w_in_mix, g_attn_out, conv_w, conv_b, conv_ln_g, conv_ln_b, w_out_mix, g_pre_ff2, g_post_ff2, ff2_w_in, ff2_w_out):
    mod = _adaln(c, w_ada, b_ada)
    h = _ffn(x, mod, g_pre_ff1, g_post_ff1, ff1_w_in.astype(BF16),
             ff1_w_out.astype(BF16), 0, 0.5)
    qkv, u = _mix_in(h, mod, g_pre_mix, w_in_mix.astype(BF16))
    a = _attention(qkv, g_attn_out)
    h = _mix_out(h, mod, a, u, conv_w, conv_b, conv_ln_g, conv_ln_b,
                 w_out_mix.astype(BF16), g_post_mix)
    h = _ffn(h, mod, g_pre_ff2, g_post_ff2, ff2_w_in.astype(BF16),
             ff2_w_out.astype(BF16), 2, 0.5)
    return h
```

```python
import functools

import jax
import jax.numpy as jnp
from jax import lax
from jax.experimental import pallas as pl
from jax.experimental.pallas import tpu as pltpu

D_MODEL = 1024
ATTN_WIDTH = D_MODEL // 2
HEAD_DIM = 64
N_HEADS = ATTN_WIDTH // HEAD_DIM
CONV_WIDTH_CH = D_MODEL - ATTN_WIDTH
CONV_KERNEL = 31
D_FF = 2816
N_SUBLAYERS = 3
MIX_IN = 3 * ATTN_WIDTH + 2 * CONV_WIDTH_CH
RMS_EPS = 1e-6
LN_EPS = 1e-5

LANES = 128
SUBLANES = 8
MXU_DIM = 256
VMEM_LIMIT = 56 * 1024 * 1024

FFN_TM = 512
FFN_CHUNK = D_FF // 2
MIX_TM = 512
OUT_TM = 256
CONV_ROWS = 32
HALO = 32
SHIFT_SPAN = HALO - SUBLANES
ATT_Q = 128
ATT_BLOCKS = 2
ATT_PAIRS = ATTN_WIDTH // LANES
ATT_K0 = 2 * ATT_Q
ATT_KB = ATT_Q
LOGW_FLOOR = -88.0
LOG2E = 1.4426950408889634

F32 = jnp.float32
BF16 = jnp.bfloat16


def _sigmoid(x):
    return 1.0 / (1.0 + jnp.exp(-x))


def _dot(a, b):
    return jnp.dot(a, b, preferred_element_type=F32)


def _rms(x, g):
    ms = jnp.mean(x * x, axis=-1, keepdims=True)
    return x * lax.rsqrt(ms + RMS_EPS) * g


def _adaln_kernel(c_ref, w_ref, b_ref, o_ref):
    c = c_ref[...]
    s = c * _sigmoid(c)
    o_ref[...] = _dot(s.astype(BF16), w_ref[...].astype(BF16)) + b_ref[...]


def _adaln(c, w_ada, b_ada):
    bsz = c.shape[0]
    n = w_ada.shape[1]
    tn = 1024
    c_pad = jnp.zeros((SUBLANES, D_MODEL), F32).at[:bsz].set(c)
    out = pl.pallas_call(
        _adaln_kernel,
        grid=(n // tn,),
        in_specs=[
            pl.BlockSpec((SUBLANES, D_MODEL), lambda j: (0, 0)),
            pl.BlockSpec((D_MODEL, tn), lambda j: (0, j)),
            pl.BlockSpec((1, tn), lambda j: (0, j)),
        ],
        out_specs=pl.BlockSpec((SUBLANES, tn), lambda j: (0, j)),
        out_shape=jax.ShapeDtypeStruct((SUBLANES, n), F32),
        name="adaln",
    )(c_pad, w_ada, b_ada.reshape(1, n))
    return out[:bsz].reshape(bsz, 3 * N_SUBLAYERS, D_MODEL)


def _modulated(x, mod_ref, sub, g_pre):
    shift = mod_ref[0, 3 * sub:3 * sub + 1, :]
    scale = mod_ref[0, 3 * sub + 1:3 * sub + 2, :]
    return _rms(x, g_pre) * (1.0 + scale) + shift


def _ffn_kernel(x_ref, mod_ref, gpre_ref, gpost_ref, win_ref, wout_ref, o_ref,
                *, sub, res_w):
    x = x_ref[0]
    hb = _modulated(x, mod_ref, sub, gpre_ref[...]).astype(BF16)
    f = None
    for c in range(D_FF // FFN_CHUNK):
        lo = c * FFN_CHUNK
        g = _dot(hb, win_ref[:, lo:lo + FFN_CHUNK])
        u = _dot(hb, win_ref[:, D_FF + lo:D_FF + lo + FFN_CHUNK])
        act = (g * _sigmoid(g) * u).astype(BF16)
        part = _dot(act, wout_ref[lo:lo + FFN_CHUNK, :])
        f = part if f is None else f + part
    y = _rms(f, gpost_ref[...])
    gate = mod_ref[0, 3 * sub + 2:3 * sub + 3, :]
    o_ref[0] = x + res_w * (1.0 + gate) * y


def _const_spec(shape):
    nd = len(shape)
    return pl.BlockSpec(shape, lambda *_: (0,) * nd, pipeline_mode=pl.Buffered(1))


def _ffn(x, mod, g_pre, g_post, w_in, w_out, sub, res_w):
    bsz, seq, d = x.shape
    tm = FFN_TM
    return pl.pallas_call(
        functools.partial(_ffn_kernel, sub=sub, res_w=res_w),
        grid=(bsz, seq // tm),
        in_specs=[
            pl.BlockSpec((1, tm, d), lambda b, i: (b, i, 0)),
            pl.BlockSpec((1, 3 * N_SUBLAYERS, d), lambda b, i: (b, 0, 0)),
            _const_spec((1, d)),
            _const_spec((1, d)),
            _const_spec((d, 2 * D_FF)),
            _const_spec((D_FF, d)),
        ],
        out_specs=pl.BlockSpec((1, tm, d), lambda b, i: (b, i, 0)),
        out_shape=jax.ShapeDtypeStruct(x.shape, F32),
        compiler_params=pltpu.CompilerParams(vmem_limit_bytes=VMEM_LIMIT),
        name=f"ffn{sub}",
    )(x, mod, g_pre.reshape(1, d), g_post.reshape(1, d), w_in, w_out)


def _mixin_kernel(x_ref, mod_ref, gpre_ref, w_ref, qkv_ref, u_ref):
    hb = _modulated(x_ref[0], mod_ref, 1, gpre_ref[...]).astype(BF16)
    aw = ATTN_WIDTH
    q = _dot(hb, w_ref[:, 0:aw]) * (HEAD_DIM ** -0.5)
    qkv_ref[0, :, 0:aw] = q.astype(BF16)
    kv = _dot(hb, w_ref[:, aw:3 * aw])
    qkv_ref[0, :, aw:3 * aw] = kv.astype(BF16)
    cv = _dot(hb, w_ref[:, 3 * aw:3 * aw + CONV_WIDTH_CH])
    cg = _dot(hb, w_ref[:, 3 * aw + CONV_WIDTH_CH:MIX_IN])
    u_ref[0] = cv * _sigmoid(cg)


def _mix_in(x, mod, g_pre, w_in_mix):
    bsz, seq, d = x.shape
    tm = MIX_TM
    return pl.pallas_call(
        _mixin_kernel,
        grid=(bsz, seq // tm),
        in_specs=[
            pl.BlockSpec((1, tm, d), lambda b, i: (b, i, 0)),
            pl.BlockSpec((1, 3 * N_SUBLAYERS, d), lambda b, i: (b, 0, 0)),
            _const_spec((1, d)),
            _const_spec((d, MIX_IN)),
        ],
        out_specs=[
            pl.BlockSpec((1, tm, 3 * ATTN_WIDTH), lambda b, i: (b, i, 0)),
            pl.BlockSpec((1, tm, CONV_WIDTH_CH), lambda b, i: (b, i, 0)),
        ],
        out_shape=[
            jax.ShapeDtypeStruct((bsz, seq, 3 * ATTN_WIDTH), BF16),
            jax.ShapeDtypeStruct((bsz, seq, CONV_WIDTH_CH), F32),
        ],
        compiler_params=pltpu.CompilerParams(vmem_limit_bytes=VMEM_LIMIT),
        name="mix_in",
    )(x, mod, g_pre.reshape(1, d), w_in_mix)


def _softplus(z):
    t = jnp.exp2(jnp.abs(z) * (-LOG2E))
    return jnp.maximum(z, 0.0) + jnp.log(1.0 + t)


def _qk(qm, k):
    return lax.dot_general(qm, k, (((1,), (1,)), ((), ())),
                           preferred_element_type=F32)


def _attn_kernel(q_ref, k_ref, v_ref, g_ref, o_ref, tri_ref, z_ref, sp_ref, after_ref):
    b = pl.program_id(0)
    step = pl.program_id(1)

    @pl.when((b == 0) & (step == 0))
    def _():
        r = lax.broadcasted_iota(jnp.int32, (ATT_K0, ATT_K0), 0)
        c = lax.broadcasted_iota(jnp.int32, (ATT_K0, ATT_K0), 1)
        tri_ref[0] = jnp.where(r > c, -1.0, 0.0).astype(BF16)
        tri_ref[1] = jnp.where((r > c) & (r >= ATT_Q), -1.0, 0.0).astype(BF16)

    first = lax.broadcasted_iota(jnp.int32, (1, LANES), 1) < HEAD_DIM
    rows2 = 2 * ATT_Q
    row = lax.broadcasted_iota(jnp.int32, (rows2, ATT_Q), 0)
    col = lax.broadcasted_iota(jnp.int32, (rows2, ATT_Q), 1)
    causal = col < jnp.bitwise_and(row, ATT_Q - 1)
    tri_b = tri_ref[0, 0:ATT_KB, 0:ATT_KB]

    def stacked_q(j, p):
        qp = q_ref[0, j * ATT_Q:(j + 1) * ATT_Q, p * LANES:(p + 1) * LANES]
        zero = jnp.zeros_like(qp)
        return jnp.concatenate(
            [jnp.where(first, qp, zero), jnp.where(first, zero, qp)], axis=0)

    def weighted_values(w, v):
        wb = w.astype(BF16)
        return jnp.where(first, _dot(wb[0:ATT_Q], v), _dot(wb[ATT_Q:rows2], v))

    def unfinished(carries):
        worst = functools.reduce(jnp.maximum, carries)
        return (jnp.max(worst) > LOGW_FLOOR).astype(jnp.int32)

    def write_normalized(j, accs):
        for p in range(ATT_PAIRS):
            lanes = slice(p * LANES, (p + 1) * LANES)
            o = accs[p]
            sq = o * o
            s_all = jnp.sum(sq, axis=-1, keepdims=True)
            s_first = jnp.sum(jnp.where(first, sq, 0.0), axis=-1, keepdims=True)
            ms = jnp.where(first, s_first, s_all - s_first) * (1.0 / HEAD_DIM)
            o_ref[0, j * ATT_Q:(j + 1) * ATT_Q, lanes] = (
                o * lax.rsqrt(ms + RMS_EPS) * g_ref[:, lanes]).astype(BF16)

    blks = [step * ATT_BLOCKS + j for j in range(ATT_BLOCKS)]
    has_prev = [blk > 0 for blk in blks]

    def window_rows(ref, j, lanes):
        prev = pl.multiple_of(jnp.maximum(blks[j] - 1, 0) * ATT_Q, ATT_Q)
        diag = pl.multiple_of(blks[j] * ATT_Q, ATT_Q)
        before = ref[0, pl.ds(prev, ATT_Q), lanes]
        before = jnp.where(has_prev[j], before, jnp.zeros_like(before))
        return jnp.concatenate([before, ref[0, pl.ds(diag, ATT_Q), lanes]], axis=0)

    for j in range(ATT_BLOCKS):
        for p in range(ATT_PAIRS):
            lanes = slice(p * LANES, (p + 1) * LANES)
            z_ref[j * ATT_PAIRS + p] = _qk(stacked_q(j, p), window_rows(k_ref, j, lanes))
    def window_softplus(z):
        sp = _softplus(z)
        return sp[:, 0:ATT_Q], jnp.where(causal, sp[:, ATT_Q:ATT_K0], 0.0), sp

    least = [[] for _ in range(ATT_BLOCKS)]
    for j in range(ATT_BLOCKS):
        tri = tri_ref[jnp.where(has_prev[j], 0, 1)]
        for p in range(ATT_PAIRS):
            slot = j * ATT_PAIRS + p
            z = z_ref[slot]
            sp_prev, sp_diag, sp = window_softplus(z)
            z_ref[slot] = z - sp
            least[j].append(jnp.min(jnp.sum(sp_prev + sp_diag, axis=-1, keepdims=True)))
            sp_ref[slot, :, 0:ATT_Q] = sp_prev.astype(BF16)
            sp_ref[slot, :, ATT_Q:ATT_K0] = sp_diag.astype(BF16)
            after_ref[slot] = _dot(sp_ref[slot], tri)
    blocks = []
    for j in range(ATT_BLOCKS):
        accs = []
        for p in range(ATT_PAIRS):
            slot = j * ATT_PAIRS + p
            lanes = slice(p * LANES, (p + 1) * LANES)
            w = jnp.exp(z_ref[slot] + after_ref[slot])
            w = jnp.concatenate(
                [w[:, 0:ATT_Q], jnp.where(causal, w[:, ATT_Q:ATT_K0], 0.0)], axis=1)
            accs.append(weighted_values(w, window_rows(v_ref, j, lanes)))
        write_normalized(j, accs)
        blocks.append((blks[j], functools.reduce(jnp.minimum, least[j]) < -LOGW_FLOOR, accs))

    for j, (blk, more, accs) in enumerate(blocks):
        @pl.when(jnp.logical_and(blk >= 2, more))
        def _(j=j, blk=blk, accs=accs):
            carries = []
            for p in range(ATT_PAIRS):
                lanes = slice(p * LANES, (p + 1) * LANES)
                z = _qk(stacked_q(j, p), window_rows(k_ref, j, lanes))
                sp_prev, sp_diag, _ = window_softplus(z)
                carries.append(-jnp.sum(sp_prev + sp_diag, axis=-1, keepdims=True))

            def cond(st):
                jb, more, _, _ = st
                return jnp.logical_and(jb >= 0, more > 0)

            def body(st):
                jb, _, carries, accs = st
                ks = pl.multiple_of(jb * ATT_KB, ATT_KB)
                new_c, new_a = [], []
                for p in range(ATT_PAIRS):
                    lanes = slice(p * LANES, (p + 1) * LANES)
                    kb = k_ref[0, pl.ds(ks, ATT_KB), lanes]
                    vb = v_ref[0, pl.ds(ks, ATT_KB), lanes]
                    z = _qk(stacked_q(j, p), kb)
                    sp = _softplus(z)
                    w = jnp.exp((z - sp) + _dot(sp.astype(BF16), tri_b) + carries[p])
                    new_a.append(accs[p] + weighted_values(w, vb))
                    new_c.append(carries[p] - jnp.sum(sp, axis=-1, keepdims=True))
                return jb - 1, unfinished(new_c), tuple(new_c), tuple(new_a)

            init = (blk - 2, jnp.int32(1), tuple(carries), tuple(accs))
            write_normalized(j, lax.while_loop(cond, body, init)[3])


def _attention(qkv, g_attn_out):
    bsz, seq, _ = qkv.shape
    tq = ATT_BLOCKS * ATT_Q
    slots = ATT_BLOCKS * ATT_PAIRS
    kv_spec = functools.partial(pl.BlockSpec, (1, seq, ATTN_WIDTH),
                                pipeline_mode=pl.Buffered(1))
    return pl.pallas_call(
        _attn_kernel,
        grid=(bsz, seq // tq),
        in_specs=[
            pl.BlockSpec((1, tq, ATTN_WIDTH), lambda b, i: (b, i, 0)),
            kv_spec(lambda b, i: (b, 0, 1)),
            kv_spec(lambda b, i: (b, 0, 2)),
            _const_spec((1, ATTN_WIDTH)),
        ],
        out_specs=pl.BlockSpec((1, tq, ATTN_WIDTH), lambda b, i: (b, i, 0)),
        out_shape=jax.ShapeDtypeStruct((bsz, seq, ATTN_WIDTH), BF16),
        scratch_shapes=[
            pltpu.VMEM((2, ATT_K0, ATT_K0), BF16),
            pltpu.VMEM((slots, 2 * ATT_Q, ATT_K0), F32),
            pltpu.VMEM((slots, 2 * ATT_Q, ATT_K0), BF16),
            pltpu.VMEM((slots, 2 * ATT_Q, ATT_K0), F32),
        ],
        compiler_params=pltpu.CompilerParams(
            dimension_semantics=("arbitrary", "arbitrary"),
            vmem_limit_bytes=VMEM_LIMIT),
        name="attn",
    )(qkv, qkv, qkv, g_attn_out.reshape(1, ATTN_WIDTH))


def _mixout_kernel(x_ref, mod_ref, a_ref, u_ref, halo_ref, cw_ref, cb_ref,
                   lng_ref, lnb_ref, w_ref, gpost_ref, o_ref, buf_ref, shift_ref,
                   act_ref):
    i = pl.program_id(1)
    tm = OUT_TM
    has_prev = (i > 0).astype(F32)
    buf_ref[0:HALO, :] = halo_ref[0] * has_prev
    buf_ref[HALO:HALO + tm, :] = u_ref[0]

    span = tm + SHIFT_SPAN
    for s in range(1, SUBLANES):
        shift_ref[s - 1] = buf_ref[s:s + span, :]

    lead = HALO - (CONV_KERNEL - 1)
    for r in range(tm // CONV_ROWS):
        groups = CONV_ROWS // SUBLANES
        y = jnp.broadcast_to(cb_ref[...][None], (groups, SUBLANES, CONV_WIDTH_CH))
        for t in range(CONV_KERNEL):
            s = (lead + t) % SUBLANES
            base = r * CONV_ROWS + lead + t - s
            if s == 0:
                rows = buf_ref[base:base + CONV_ROWS, :]
            else:
                rows = shift_ref[s - 1, base:base + CONV_ROWS, :]
            y = y + cw_ref[t][None] * rows.reshape(groups, SUBLANES, CONV_WIDTH_CH)
        y = y.reshape(CONV_ROWS, CONV_WIDTH_CH)
        mu = jnp.mean(y, axis=-1, keepdims=True)
        yc = y - mu
        var = jnp.mean(yc * yc, axis=-1, keepdims=True)
        yn = yc * lax.rsqrt(var + LN_EPS) * lng_ref[...] + lnb_ref[...]
        act_ref[r * CONV_ROWS:(r + 1) * CONV_ROWS, :] = (yn * _sigmoid(yn)).astype(BF16)

    m = _dot(a_ref[0], w_ref[0:ATTN_WIDTH, :]) + _dot(act_ref[...], w_ref[ATTN_WIDTH:D_MODEL, :])
    y = _rms(m, gpost_ref[...])
    gate = mod_ref[0, 5:6, :]
    o_ref[0] = x_ref[0] + (1.0 + gate) * y


def _mix_out(x, mod, a, u, conv_w, conv_b, ln_g, ln_b, w_out_mix, g_post):
    bsz, seq, d = x.shape
    tm = OUT_TM
    cw = CONV_WIDTH_CH
    per = tm // HALO
    return pl.pallas_call(
        _mixout_kernel,
        grid=(bsz, seq // tm),
        in_specs=[
            pl.BlockSpec((1, tm, d), lambda b, i: (b, i, 0)),
            pl.BlockSpec((1, 3 * N_SUBLAYERS, d), lambda b, i: (b, 0, 0)),
            pl.BlockSpec((1, tm, ATTN_WIDTH), lambda b, i: (b, i, 0)),
            pl.BlockSpec((1, tm, cw), lambda b, i: (b, i, 0)),
            pl.BlockSpec((1, HALO, cw), lambda b, i: (b, jnp.maximum(i * per - 1, 0), 0)),
            _const_spec((CONV_KERNEL, SUBLANES, cw)),
            _const_spec((SUBLANES, cw)),
            _const_spec((1, cw)),
            _const_spec((1, cw)),
            _const_spec((d, d)),
            _const_spec((1, d)),
        ],
        out_specs=pl.BlockSpec((1, tm, d), lambda b, i: (b, i, 0)),
        out_shape=jax.ShapeDtypeStruct(x.shape, F32),
        scratch_shapes=[pltpu.VMEM((HALO + tm, cw), F32),
                        pltpu.VMEM((SUBLANES - 1, tm + SHIFT_SPAN, cw), F32),
                        pltpu.VMEM((tm, cw), BF16)],
        compiler_params=pltpu.CompilerParams(vmem_limit_bytes=VMEM_LIMIT),
        name="mix_out",
    )(x, mod, a, u, u,
      jnp.broadcast_to(conv_w[:, None, :], (CONV_KERNEL, SUBLANES, cw)),
      jnp.broadcast_to(conv_b[None, :], (SUBLANES, cw)), ln_g.reshape(1, cw),
      ln_b.reshape(1, cw), w_out_mix, g_post.reshape(1, d))


def kernel(x, c, w_ada, b_ada, g_pre_ff1, g_post_ff1, ff1_w_in, ff1_w_out, g_pre_mix, g_post_mix, w_in_mix, g_attn_out, conv_w, conv_b, conv_ln_g, conv_ln_b, w_out_mix, g_pre_ff2, g_post_ff2, ff2_w_in, ff2_w_out):
    mod = _adaln(c, w_ada, b_ada)
    h = _ffn(x, mod, g_pre_ff1, g_post_ff1, ff1_w_in.astype(BF16),
             ff1_w_out.astype(BF16), 0, 0.5)
    qkv, u = _mix_in(h, mod, g_pre_mix, w_in_mix.astype(BF16))
    a = _attention(qkv, g_attn_out)
    h = _mix_out(h, mod, a, u, conv_w, conv_b, conv_ln_g, conv_ln_b,
                 w_out_mix.astype(BF16), g_post_mix)
    h = _ffn(h, mod, g_pre_ff2, g_post_ff2, ff2_w_in.astype(BF16),
             ff2_w_out.astype(BF16), 2, 0.5)
    return h
```

```python
import functools

import jax
import jax.numpy as jnp
from jax import lax
from jax.experimental import pallas as pl
from jax.experimental.pallas import tpu as pltpu

D_MODEL = 1024
ATTN_WIDTH = D_MODEL // 2
HEAD_DIM = 64
N_HEADS = ATTN_WIDTH // HEAD_DIM
CONV_WIDTH_CH = D_MODEL - ATTN_WIDTH
CONV_KERNEL = 31
D_FF = 2816
N_SUBLAYERS = 3
MIX_IN = 3 * ATTN_WIDTH + 2 * CONV_WIDTH_CH
RMS_EPS = 1e-6
LN_EPS = 1e-5

LANES = 128
SUBLANES = 8
MXU_DIM = 256
VMEM_LIMIT = 56 * 1024 * 1024

FFN_TM = 512
FFN_CHUNK = D_FF // 2
MIX_TM = 512
OUT_TM = 256
CONV_ROWS = 32
HALO = 32
SHIFT_SPAN = HALO - SUBLANES
ATT_Q = 64
ATT_BLOCKS = 4
ATT_PAIRS = ATTN_WIDTH // LANES
ATT_K0 = 256
ATT_BACK = ATT_K0 - ATT_Q
ATT_KB = ATT_Q
LOGW_FLOOR = -88.0
LOG2E = 1.4426950408889634

F32 = jnp.float32
BF16 = jnp.bfloat16


def _sigmoid(x):
    return 1.0 / (1.0 + jnp.exp(-x))


def _dot(a, b):
    return jnp.dot(a, b, preferred_element_type=F32)


def _rms(x, g):
    ms = jnp.mean(x * x, axis=-1, keepdims=True)
    return x * lax.rsqrt(ms + RMS_EPS) * g


def _adaln_kernel(c_ref, w_ref, b_ref, o_ref):
    c = c_ref[...]
    s = c * _sigmoid(c)
    o_ref[...] = _dot(s.astype(BF16), w_ref[...].astype(BF16)) + b_ref[...]


def _adaln(c, w_ada, b_ada):
    bsz = c.shape[0]
    n = w_ada.shape[1]
    tn = 1024
    c_pad = jnp.zeros((SUBLANES, D_MODEL), F32).at[:bsz].set(c)
    out = pl.pallas_call(
        _adaln_kernel,
        grid=(n // tn,),
        in_specs=[
            pl.BlockSpec((SUBLANES, D_MODEL), lambda j: (0, 0)),
            pl.BlockSpec((D_MODEL, tn), lambda j: (0, j)),
            pl.BlockSpec((1, tn), lambda j: (0, j)),
        ],
        out_specs=pl.BlockSpec((SUBLANES, tn), lambda j: (0, j)),
        out_shape=jax.ShapeDtypeStruct((SUBLANES, n), F32),
        name="adaln",
    )(c_pad, w_ada, b_ada.reshape(1, n))
    return out[:bsz].reshape(bsz, 3 * N_SUBLAYERS, D_MODEL)


def _modulated(x, mod_ref, sub, g_pre):
    shift = mod_ref[0, 3 * sub:3 * sub + 1, :]
    scale = mod_ref[0, 3 * sub + 1:3 * sub + 2, :]
    return _rms(x, g_pre) * (1.0 + scale) + shift


def _ffn_kernel(x_ref, mod_ref, gpre_ref, gpost_ref, win_ref, wout_ref, o_ref,
                *, sub, res_w):
    x = x_ref[0]
    hb = _modulated(x, mod_ref, sub, gpre_ref[...]).astype(BF16)
    f = None
    for c in range(D_FF // FFN_CHUNK):
        lo = c * FFN_CHUNK
        g = _dot(hb, win_ref[:, lo:lo + FFN_CHUNK])
        u = _dot(hb, win_ref[:, D_FF + lo:D_FF + lo + FFN_CHUNK])
        act = (g * _sigmoid(g) * u).astype(BF16)
        part = _dot(act, wout_ref[lo:lo + FFN_CHUNK, :])
        f = part if f is None else f + part
    y = _rms(f, gpost_ref[...])
    gate = mod_ref[0, 3 * sub + 2:3 * sub + 3, :]
    o_ref[0] = x + res_w * (1.0 + gate) * y


def _const_spec(shape):
    nd = len(shape)
    return pl.BlockSpec(shape, lambda *_: (0,) * nd, pipeline_mode=pl.Buffered(1))


def _ffn(x, mod, g_pre, g_post, w_in, w_out, sub, res_w):
    bsz, seq, d = x.shape
    tm = FFN_TM
    return pl.pallas_call(
        functools.partial(_ffn_kernel, sub=sub, res_w=res_w),
        grid=(bsz, seq // tm),
        in_specs=[
            pl.BlockSpec((1, tm, d), lambda b, i: (b, i, 0)),
            pl.BlockSpec((1, 3 * N_SUBLAYERS, d), lambda b, i: (b, 0, 0)),
            _const_spec((1, d)),
            _const_spec((1, d)),
            _const_spec((d, 2 * D_FF)),
            _const_spec((D_FF, d)),
        ],
        out_specs=pl.BlockSpec((1, tm, d), lambda b, i: (b, i, 0)),
        out_shape=jax.ShapeDtypeStruct(x.shape, F32),
        compiler_params=pltpu.CompilerParams(vmem_limit_bytes=VMEM_LIMIT),
        name=f"ffn{sub}",
    )(x, mod, g_pre.reshape(1, d), g_post.reshape(1, d), w_in, w_out)


def _mixin_kernel(x_ref, mod_ref, gpre_ref, w_ref, qkv_ref, u_ref):
    hb = _modulated(x_ref[0], mod_ref, 1, gpre_ref[...]).astype(BF16)
    aw = ATTN_WIDTH
    q = _dot(hb, w_ref[:, 0:aw]) * (HEAD_DIM ** -0.5)
    qkv_ref[0, :, 0:aw] = q.astype(BF16)
    kv = _dot(hb, w_ref[:, aw:3 * aw])
    qkv_ref[0, :, aw:3 * aw] = kv.astype(BF16)
    cv = _dot(hb, w_ref[:, 3 * aw:3 * aw + CONV_WIDTH_CH])
    cg = _dot(hb, w_ref[:, 3 * aw + CONV_WIDTH_CH:MIX_IN])
    u_ref[0] = cv * _sigmoid(cg)


def _mix_in(x, mod, g_pre, w_in_mix):
    bsz, seq, d = x.shape
    tm = MIX_TM
    return pl.pallas_call(
        _mixin_kernel,
        grid=(bsz, seq // tm),
        in_specs=[
            pl.BlockSpec((1, tm, d), lambda b, i: (b, i, 0)),
            pl.BlockSpec((1, 3 * N_SUBLAYERS, d), lambda b, i: (b, 0, 0)),
            _const_spec((1, d)),
            _const_spec((d, MIX_IN)),
        ],
        out_specs=[
            pl.BlockSpec((1, tm, 3 * ATTN_WIDTH), lambda b, i: (b, i, 0)),
            pl.BlockSpec((1, tm, CONV_WIDTH_CH), lambda b, i: (b, i, 0)),
        ],
        out_shape=[
            jax.ShapeDtypeStruct((bsz, seq, 3 * ATTN_WIDTH), BF16),
            jax.ShapeDtypeStruct((bsz, seq, CONV_WIDTH_CH), F32),
        ],
        compiler_params=pltpu.CompilerParams(vmem_limit_bytes=VMEM_LIMIT),
        name="mix_in",
    )(x, mod, g_pre.reshape(1, d), w_in_mix)


def _softplus(z):
    t = jnp.exp2(jnp.abs(z) * (-LOG2E))
    return jnp.maximum(z, 0.0) + jnp.log(1.0 + t)


def _qk(qm, k):
    return lax.dot_general(qm, k, (((1,), (1,)), ((), ())),
                           preferred_element_type=F32)


def _attn_kernel(q_ref, k_ref, v_ref, g_ref, o_ref, tri_ref, z_ref, sp_ref, after_ref):
    b = pl.program_id(0)
    step = pl.program_id(1)
    back_chunks = ATT_BACK // ATT_Q

    @pl.when((b == 0) & (step == 0))
    def _():
        r = lax.broadcasted_iota(jnp.int32, (ATT_K0, ATT_K0), 0)
        c = lax.broadcasted_iota(jnp.int32, (ATT_K0, ATT_K0), 1)
        for missing in range(back_chunks + 1):
            dead = (r >= ATT_BACK - missing * ATT_Q) & (r < ATT_BACK)
            tri_ref[missing] = jnp.where((r > c) & jnp.logical_not(dead), -1.0, 0.0).astype(BF16)

    first = lax.broadcasted_iota(jnp.int32, (1, LANES), 1) < HEAD_DIM
    rows2 = 2 * ATT_Q
    row = lax.broadcasted_iota(jnp.int32, (rows2, LANES), 0)
    col = lax.broadcasted_iota(jnp.int32, (rows2, LANES), 1) + (ATT_K0 - LANES)
    causal = (col - ATT_BACK) < jnp.bitwise_and(row, ATT_Q - 1)
    tri_b = tri_ref[0, 0:ATT_KB, 0:ATT_KB]

    def stacked_q(j, p):
        qp = q_ref[0, j * ATT_Q:(j + 1) * ATT_Q, p * LANES:(p + 1) * LANES]
        zero = jnp.zeros_like(qp)
        return jnp.concatenate(
            [jnp.where(first, qp, zero), jnp.where(first, zero, qp)], axis=0)

    def weighted_values(w, v):
        wb = w.astype(BF16)
        return jnp.where(first, _dot(wb[0:ATT_Q], v), _dot(wb[ATT_Q:rows2], v))

    def unfinished(carries):
        worst = functools.reduce(jnp.maximum, carries)
        return (jnp.max(worst) > LOGW_FLOOR).astype(jnp.int32)

    def write_normalized(j, accs):
        for p in range(ATT_PAIRS):
            lanes = slice(p * LANES, (p + 1) * LANES)
            o = accs[p]
            sq = o * o
            s_all = jnp.sum(sq, axis=-1, keepdims=True)
            s_first = jnp.sum(jnp.where(first, sq, 0.0), axis=-1, keepdims=True)
            ms = jnp.where(first, s_first, s_all - s_first) * (1.0 / HEAD_DIM)
            o_ref[0, j * ATT_Q:(j + 1) * ATT_Q, lanes] = (
                o * lax.rsqrt(ms + RMS_EPS) * g_ref[:, lanes]).astype(BF16)

    blks = [step * ATT_BLOCKS + j for j in range(ATT_BLOCKS)]
    back_rows = [pl.multiple_of(jnp.maximum(blk * ATT_Q - ATT_BACK, 0), ATT_Q) for blk in blks]
    diag_rows = [pl.multiple_of(blk * ATT_Q, ATT_Q) for blk in blks]

    def window_keys(j, lanes):
        return jnp.concatenate([k_ref[0, pl.ds(back_rows[j], ATT_BACK), lanes],
                                k_ref[0, pl.ds(diag_rows[j], ATT_Q), lanes]], axis=0)

    def window_values(j, lanes):
        chunks = []
        for i in range(back_chunks):
            rows = pl.multiple_of(back_rows[j] + i * ATT_Q, ATT_Q)
            chunk = v_ref[0, pl.ds(rows, ATT_Q), lanes]
            chunks.append(jnp.where(blks[j] > i, chunk, jnp.zeros_like(chunk)))
        chunks.append(v_ref[0, pl.ds(diag_rows[j], ATT_Q), lanes])
        return jnp.concatenate(chunks, axis=0)

    def window_softplus(z):
        sp = _softplus(z)
        return sp[:, 0:LANES], jnp.where(causal, sp[:, LANES:ATT_K0], 0.0), sp

    for j in range(ATT_BLOCKS):
        for p in range(ATT_PAIRS):
            lanes = slice(p * LANES, (p + 1) * LANES)
            z_ref[j * ATT_PAIRS + p] = _qk(stacked_q(j, p), window_keys(j, lanes))
    least = [[] for _ in range(ATT_BLOCKS)]
    for j in range(ATT_BLOCKS):
        tri = tri_ref[jnp.maximum(back_chunks - blks[j], 0)]
        for p in range(ATT_PAIRS):
            slot = j * ATT_PAIRS + p
            z = z_ref[slot]
            sp_lo, sp_hi, sp = window_softplus(z)
            z_ref[slot] = z - sp
            least[j].append(jnp.min(jnp.sum(sp_lo + sp_hi, axis=-1, keepdims=True)))
            sp_ref[slot, :, 0:LANES] = sp_lo.astype(BF16)
            sp_ref[slot, :, LANES:ATT_K0] = sp_hi.astype(BF16)
            after_ref[slot] = _dot(sp_ref[slot], tri)
    blocks = []
    for j in range(ATT_BLOCKS):
        accs = []
        for p in range(ATT_PAIRS):
            slot = j * ATT_PAIRS + p
            lanes = slice(p * LANES, (p + 1) * LANES)
            w = jnp.exp(z_ref[slot] + after_ref[slot])
            w = jnp.concatenate(
                [w[:, 0:LANES], jnp.where(causal, w[:, LANES:ATT_K0], 0.0)], axis=1)
            accs.append(weighted_values(w, window_values(j, lanes)))
        write_normalized(j, accs)
        blocks.append((blks[j], functools.reduce(jnp.minimum, least[j]) < -LOGW_FLOOR, accs))

    for j, (blk, more, accs) in enumerate(blocks):
        @pl.when(jnp.logical_and(blk > back_chunks, more))
        def _(j=j, blk=blk, accs=accs):
            carries = []
            for p in range(ATT_PAIRS):
                lanes = slice(p * LANES, (p + 1) * LANES)
                z = _qk(stacked_q(j, p), window_keys(j, lanes))
                sp_lo, sp_hi, _ = window_softplus(z)
                carries.append(-jnp.sum(sp_lo + sp_hi, axis=-1, keepdims=True))

            def cond(st):
                jb, more, _, _ = st
                return jnp.logical_and(jb >= 0, more > 0)

            def body(st):
                jb, _, carries, accs = st
                ks = pl.multiple_of(jb * ATT_KB, ATT_KB)
                new_c, new_a = [], []
                for p in range(ATT_PAIRS):
                    lanes = slice(p * LANES, (p + 1) * LANES)
                    kb = k_ref[0, pl.ds(ks, ATT_KB), lanes]
                    vb = v_ref[0, pl.ds(ks, ATT_KB), lanes]
                    z = _qk(stacked_q(j, p), kb)
                    sp = _softplus(z)
                    w = jnp.exp((z - sp) + _dot(sp.astype(BF16), tri_b) + carries[p])
                    new_a.append(accs[p] + weighted_values(w, vb))
                    new_c.append(carries[p] - jnp.sum(sp, axis=-1, keepdims=True))
                return jb - 1, unfinished(new_c), tuple(new_c), tuple(new_a)

            init = (blk - back_chunks - 1, jnp.int32(1), tuple(carries), tuple(accs))
            write_normalized(j, lax.while_loop(cond, body, init)[3])


def _attention(qkv, g_attn_out):
    bsz, seq, _ = qkv.shape
    tq = ATT_BLOCKS * ATT_Q
    slots = ATT_BLOCKS * ATT_PAIRS
    kv_spec = functools.partial(pl.BlockSpec, (1, seq, ATTN_WIDTH),
                                pipeline_mode=pl.Buffered(1))
    return pl.pallas_call(
        _attn_kernel,
        grid=(bsz, seq // tq),
        in_specs=[
            pl.BlockSpec((1, tq, ATTN_WIDTH), lambda b, i: (b, i, 0)),
            kv_spec(lambda b, i: (b, 0, 1)),
            kv_spec(lambda b, i: (b, 0, 2)),
            _const_spec((1, ATTN_WIDTH)),
        ],
        out_specs=pl.BlockSpec((1, tq, ATTN_WIDTH), lambda b, i: (b, i, 0)),
        out_shape=jax.ShapeDtypeStruct((bsz, seq, ATTN_WIDTH), BF16),
        scratch_shapes=[
            pltpu.VMEM((ATT_BACK // ATT_Q + 1, ATT_K0, ATT_K0), BF16),
            pltpu.VMEM((slots, 2 * ATT_Q, ATT_K0), F32),
            pltpu.VMEM((slots, 2 * ATT_Q, ATT_K0), BF16),
            pltpu.VMEM((slots, 2 * ATT_Q, ATT_K0), F32),
        ],
        compiler_params=pltpu.CompilerParams(
            dimension_semantics=("arbitrary", "arbitrary"),
            vmem_limit_bytes=VMEM_LIMIT),
        name="attn",
    )(qkv, qkv, qkv, g_attn_out.reshape(1, ATTN_WIDTH))


def _mixout_kernel(x_ref, mod_ref, a_ref, u_ref, halo_ref, cw_ref, cb_ref,
                   lng_ref, lnb_ref, w_ref, gpost_ref, o_ref, buf_ref, shift_ref,
                   act_ref):
    i = pl.program_id(1)
    tm = OUT_TM
    has_prev = (i > 0).astype(F32)
    buf_ref[0:HALO, :] = halo_ref[0] * has_prev
    buf_ref[HALO:HALO + tm, :] = u_ref[0]

    span = tm + SHIFT_SPAN
    for s in range(1, SUBLANES):
        shift_ref[s - 1] = buf_ref[s:s + span, :]

    lead = HALO - (CONV_KERNEL - 1)
    for r in range(tm // CONV_ROWS):
        groups = CONV_ROWS // SUBLANES
        y = jnp.broadcast_to(cb_ref[...][None], (groups, SUBLANES, CONV_WIDTH_CH))
        for t in range(CONV_KERNEL):
            s = (lead + t) % SUBLANES
            base = r * CONV_ROWS + lead + t - s
            if s == 0:
                rows = buf_ref[base:base + CONV_ROWS, :]
            else:
                rows = shift_ref[s - 1, base:base + CONV_ROWS, :]
            y = y + cw_ref[t][None] * rows.reshape(groups, SUBLANES, CONV_WIDTH_CH)
        y = y.reshape(CONV_ROWS, CONV_WIDTH_CH)
        mu = jnp.mean(y, axis=-1, keepdims=True)
        yc = y - mu
        var = jnp.mean(yc * yc, axis=-1, keepdims=True)
        yn = yc * lax.rsqrt(var + LN_EPS) * lng_ref[...] + lnb_ref[...]
        act_ref[r * CONV_ROWS:(r + 1) * CONV_ROWS, :] = (yn * _sigmoid(yn)).astype(BF16)

    m = _dot(a_ref[0], w_ref[0:ATTN_WIDTH, :]) + _dot(act_ref[...], w_ref[ATTN_WIDTH:D_MODEL, :])
    y = _rms(m, gpost_ref[...])
    gate = mod_ref[0, 5:6, :]
    o_ref[0] = x_ref[0] + (1.0 + gate) * y


def _mix_out(x, mod, a, u, conv_w, conv_b, ln_g, ln_b, w_out_mix, g_post):
    bsz, seq, d = x.shape
    tm = OUT_TM
    cw = CONV_WIDTH_CH
    per = tm // HALO
    return pl.pallas_call(
        _mixout_kernel,
        grid=(bsz, seq // tm),
        in_specs=[
            pl.BlockSpec((1, tm, d), lambda b, i: (b, i, 0)),
            pl.BlockSpec((1, 3 * N_SUBLAYERS, d), lambda b, i: (b, 0, 0)),
            pl.BlockSpec((1, tm, ATTN_WIDTH), lambda b, i: (b, i, 0)),
            pl.BlockSpec((1, tm, cw), lambda b, i: (b, i, 0)),
            pl.BlockSpec((1, HALO, cw), lambda b, i: (b, jnp.maximum(i * per - 1, 0), 0)),
            _const_spec((CONV_KERNEL, SUBLANES, cw)),
            _const_spec((SUBLANES, cw)),
            _const_spec((1, cw)),
            _const_spec((1, cw)),
            _const_spec((d, d)),
            _const_spec((1, d)),
        ],
        out_specs=pl.BlockSpec((1, tm, d), lambda b, i: (b, i, 0)),
        out_shape=jax.ShapeDtypeStruct(x.shape, F32),
        scratch_shapes=[pltpu.VMEM((HALO + tm, cw), F32),
                        pltpu.VMEM((SUBLANES - 1, tm + SHIFT_SPAN, cw), F32),
                        pltpu.VMEM((tm, cw), BF16)],
        compiler_params=pltpu.CompilerParams(vmem_limit_bytes=VMEM_LIMIT),
        name="mix_out",
    )(x, mod, a, u, u,
      jnp.broadcast_to(conv_w[:, None, :], (CONV_KERNEL, SUBLANES, cw)),
      jnp.broadcast_to(conv_b[None, :], (SUBLANES, cw)), ln_g.reshape(1, cw),
      ln_b.reshape(1, cw), w_out_mix, g_post.reshape(1, d))


def kernel(x, c, w_ada, b_ada, g_pre_ff1, g_post_ff1, ff1_w_in, ff1_w_out, g_pre_mix, g_post_mix, w_in_mix, g_attn_out, conv_w, conv_b, conv_ln_g, conv_ln_b, w_out_mix, g_pre_ff2, g_post_ff2, ff2_w_in, ff2_w_out):
    mod = _adaln(c, w_ada, b_ada)
    h = _ffn(x, mod, g_pre_ff1, g_post_ff1, ff1_w_in.astype(BF16),
             ff1_w_out.astype(BF16), 0, 0.5)
    qkv, u = _mix_in(h, mod, g_pre_mix, w_in_mix.astype(BF16))
    a = _attention(qkv, g_attn_out)
    h = _mix_out(h, mod, a, u, conv_w, conv_b, conv_ln_g, conv_ln_b,
                 w_out_mix.astype(BF16), g_post_mix)
    h = _ffn(h, mod, g_pre_ff2, g_post_ff2, ff2_w_in.astype(BF16),
             ff2_w_out.astype(BF16), 2, 0.5)
    return h
```

```python
import functools

import jax
import jax.numpy as jnp
from jax import lax
from jax.experimental import pallas as pl
from jax.experimental.pallas import tpu as pltpu

D_MODEL = 1024
ATTN_WIDTH = D_MODEL // 2
HEAD_DIM = 64
N_HEADS = ATTN_WIDTH // HEAD_DIM
CONV_WIDTH_CH = D_MODEL - ATTN_WIDTH
CONV_KERNEL = 31
D_FF = 2816
N_SUBLAYERS = 3
MIX_IN = 3 * ATTN_WIDTH + 2 * CONV_WIDTH_CH
RMS_EPS = 1e-6
LN_EPS = 1e-5

LANES = 128
SUBLANES = 8
MXU_DIM = 256
VMEM_LIMIT = 56 * 1024 * 1024

FFN_TM = 1024
FFN_SUB = 256
FFN_CHUNK = D_FF // 2
MIX_TM = 1024
MIX_SUB = 256
MIXOUT_TM = 256
MIXOUT_SUB = 256
CONV_ROWS = 32
HALO = 32
SHIFT_SPAN = HALO - SUBLANES
ATT_Q = 64
ATT_BLOCKS = 4
ATT_PAIRS = ATTN_WIDTH // LANES
ATT_K0 = 256
ATT_BACK = ATT_K0 - ATT_Q
ATT_KB = ATT_Q
LOGW_FLOOR = -88.0
LOG2E = 1.4426950408889634

F32 = jnp.float32
BF16 = jnp.bfloat16


def _sigmoid(x):
    return 1.0 / (1.0 + jnp.exp(-x))


def _dot(a, b):
    return jnp.dot(a, b, preferred_element_type=F32)


def _rms(x, g):
    ms = jnp.mean(x * x, axis=-1, keepdims=True)
    return x * lax.rsqrt(ms + RMS_EPS) * g


def _adaln_kernel(c_ref, w_ref, b_ref, o_ref):
    c = c_ref[...]
    s = c * _sigmoid(c)
    o_ref[...] = _dot(s.astype(BF16), w_ref[...].astype(BF16)) + b_ref[...]


def _adaln(c, w_ada, b_ada):
    bsz = c.shape[0]
    n = w_ada.shape[1]
    tn = 1024
    c_pad = jnp.zeros((SUBLANES, D_MODEL), F32).at[:bsz].set(c)
    out = pl.pallas_call(
        _adaln_kernel,
        grid=(n // tn,),
        in_specs=[
            pl.BlockSpec((SUBLANES, D_MODEL), lambda j: (0, 0)),
            pl.BlockSpec((D_MODEL, tn), lambda j: (0, j)),
            pl.BlockSpec((1, tn), lambda j: (0, j)),
        ],
        out_specs=pl.BlockSpec((SUBLANES, tn), lambda j: (0, j)),
        out_shape=jax.ShapeDtypeStruct((SUBLANES, n), F32),
        name="adaln",
    )(c_pad, w_ada, b_ada.reshape(1, n))
    return out[:bsz].reshape(bsz, 3 * N_SUBLAYERS, D_MODEL)


def _modulated(x, mod_ref, sub, g_pre):
    shift = mod_ref[0, 3 * sub:3 * sub + 1, :]
    scale = mod_ref[0, 3 * sub + 1:3 * sub + 2, :]
    return _rms(x, g_pre) * (1.0 + scale) + shift


def _ffn_rows(x, mod_ref, gpre_ref, gpost_ref, win_ref, wout_ref, sub, res_w):
    hb = _modulated(x, mod_ref, sub, gpre_ref[...]).astype(BF16)
    f = None
    for c in range(D_FF // FFN_CHUNK):
        lo = c * FFN_CHUNK
        g = _dot(hb, win_ref[:, lo:lo + FFN_CHUNK])
        u = _dot(hb, win_ref[:, D_FF + lo:D_FF + lo + FFN_CHUNK])
        act = (g * _sigmoid(g) * u).astype(BF16)
        part = _dot(act, wout_ref[lo:lo + FFN_CHUNK, :])
        f = part if f is None else f + part
    y = _rms(f, gpost_ref[...])
    gate = mod_ref[0, 3 * sub + 2:3 * sub + 3, :]
    return x + res_w * (1.0 + gate) * y


def _ffn_kernel(x_ref, mod_ref, gpre_ref, gpost_ref, win_ref, wout_ref, o_ref,
                *, sub, res_w):
    for s in range(FFN_TM // FFN_SUB):
        rows = slice(s * FFN_SUB, (s + 1) * FFN_SUB)
        o_ref[0, rows, :] = _ffn_rows(x_ref[0, rows, :], mod_ref, gpre_ref, gpost_ref,
                                      win_ref, wout_ref, sub, res_w)


def _const_spec(shape):
    nd = len(shape)
    return pl.BlockSpec(shape, lambda *_: (0,) * nd, pipeline_mode=pl.Buffered(1))


def _ffn(x, mod, g_pre, g_post, w_in, w_out, sub, res_w):
    bsz, seq, d = x.shape
    tm = FFN_TM
    return pl.pallas_call(
        functools.partial(_ffn_kernel, sub=sub, res_w=res_w),
        grid=(bsz, seq // tm),
        in_specs=[
            pl.BlockSpec((1, tm, d), lambda b, i: (b, i, 0)),
            pl.BlockSpec((1, 3 * N_SUBLAYERS, d), lambda b, i: (b, 0, 0)),
            _const_spec((1, d)),
            _const_spec((1, d)),
            _const_spec((d, 2 * D_FF)),
            _const_spec((D_FF, d)),
        ],
        out_specs=pl.BlockSpec((1, tm, d), lambda b, i: (b, i, 0)),
        out_shape=jax.ShapeDtypeStruct(x.shape, F32),
        compiler_params=pltpu.CompilerParams(vmem_limit_bytes=VMEM_LIMIT),
        name=f"ffn{sub}",
    )(x, mod, g_pre.reshape(1, d), g_post.reshape(1, d), w_in, w_out)


def _mixin_kernel(x_ref, mod_ref, gpre_ref, w_ref, qkv_ref, u_ref):
    aw = ATTN_WIDTH
    for s in range(MIX_TM // MIX_SUB):
        rows = slice(s * MIX_SUB, (s + 1) * MIX_SUB)
        hb = _modulated(x_ref[0, rows, :], mod_ref, 1, gpre_ref[...]).astype(BF16)
        q = _dot(hb, w_ref[:, 0:aw]) * (HEAD_DIM ** -0.5)
        qkv_ref[0, rows, 0:aw] = q.astype(BF16)
        kv = _dot(hb, w_ref[:, aw:3 * aw])
        qkv_ref[0, rows, aw:3 * aw] = kv.astype(BF16)
        cv = _dot(hb, w_ref[:, 3 * aw:3 * aw + CONV_WIDTH_CH])
        cg = _dot(hb, w_ref[:, 3 * aw + CONV_WIDTH_CH:MIX_IN])
        u_ref[0, rows, :] = cv * _sigmoid(cg)


def _mix_in(x, mod, g_pre, w_in_mix):
    bsz, seq, d = x.shape
    tm = MIX_TM
    return pl.pallas_call(
        _mixin_kernel,
        grid=(bsz, seq // tm),
        in_specs=[
            pl.BlockSpec((1, tm, d), lambda b, i: (b, i, 0)),
            pl.BlockSpec((1, 3 * N_SUBLAYERS, d), lambda b, i: (b, 0, 0)),
            _const_spec((1, d)),
            _const_spec((d, MIX_IN)),
        ],
        out_specs=[
            pl.BlockSpec((1, tm, 3 * ATTN_WIDTH), lambda b, i: (b, i, 0)),
            pl.BlockSpec((1, tm, CONV_WIDTH_CH), lambda b, i: (b, i, 0)),
        ],
        out_shape=[
            jax.ShapeDtypeStruct((bsz, seq, 3 * ATTN_WIDTH), BF16),
            jax.ShapeDtypeStruct((bsz, seq, CONV_WIDTH_CH), F32),
        ],
        compiler_params=pltpu.CompilerParams(vmem_limit_bytes=VMEM_LIMIT),
        name="mix_in",
    )(x, mod, g_pre.reshape(1, d), w_in_mix)


def _softplus(z):
    t = jnp.exp2(jnp.abs(z) * (-LOG2E))
    return jnp.maximum(z, 0.0) + jnp.log(1.0 + t)


def _qk(qm, k):
    return lax.dot_general(qm, k, (((1,), (1,)), ((), ())),
                           preferred_element_type=F32)


def _attn_kernel(q_ref, k_ref, v_ref, g_ref, o_ref, tri_ref, z_ref, sp_ref, after_ref):
    b = pl.program_id(0)
    step = pl.program_id(1)
    back_chunks = ATT_BACK // ATT_Q

    @pl.when((b == 0) & (step == 0))
    def _():
        r = lax.broadcasted_iota(jnp.int32, (ATT_K0, ATT_K0), 0)
        c = lax.broadcasted_iota(jnp.int32, (ATT_K0, ATT_K0), 1)
        for missing in range(back_chunks + 1):
            dead = (r >= ATT_BACK - missing * ATT_Q) & (r < ATT_BACK)
            tri_ref[missing] = jnp.where((r > c) & jnp.logical_not(dead), -1.0, 0.0).astype(BF16)

    first = lax.broadcasted_iota(jnp.int32, (1, LANES), 1) < HEAD_DIM
    rows2 = 2 * ATT_Q
    row = lax.broadcasted_iota(jnp.int32, (rows2, LANES), 0)
    col = lax.broadcasted_iota(jnp.int32, (rows2, LANES), 1) + (ATT_K0 - LANES)
    causal = (col - ATT_BACK) < jnp.bitwise_and(row, ATT_Q - 1)
    tri_b = tri_ref[0, 0:ATT_KB, 0:ATT_KB]

    def stacked_q(j, p):
        qp = q_ref[0, j * ATT_Q:(j + 1) * ATT_Q, p * LANES:(p + 1) * LANES]
        zero = jnp.zeros_like(qp)
        return jnp.concatenate(
            [jnp.where(first, qp, zero), jnp.where(first, zero, qp)], axis=0)

    def weighted_values(w, v):
        wb = w.astype(BF16)
        return jnp.where(first, _dot(wb[0:ATT_Q], v), _dot(wb[ATT_Q:rows2], v))

    def unfinished(carries):
        worst = functools.reduce(jnp.maximum, carries)
        return (jnp.max(worst) > LOGW_FLOOR).astype(jnp.int32)

    def write_normalized(j, accs):
        for p in range(ATT_PAIRS):
            lanes = slice(p * LANES, (p + 1) * LANES)
            o = accs[p]
            sq = o * o
            s_all = jnp.sum(sq, axis=-1, keepdims=True)
            s_first = jnp.sum(jnp.where(first, sq, 0.0), axis=-1, keepdims=True)
            ms = jnp.where(first, s_first, s_all - s_first) * (1.0 / HEAD_DIM)
            o_ref[0, j * ATT_Q:(j + 1) * ATT_Q, lanes] = (
                o * lax.rsqrt(ms + RMS_EPS) * g_ref[:, lanes]).astype(BF16)

    blks = [step * ATT_BLOCKS + j for j in range(ATT_BLOCKS)]
    back_rows = [pl.multiple_of(jnp.maximum(blk * ATT_Q - ATT_BACK, 0), ATT_Q) for blk in blks]
    diag_rows = [pl.multiple_of(blk * ATT_Q, ATT_Q) for blk in blks]

    def window_keys(j, lanes):
        return jnp.concatenate([k_ref[0, pl.ds(back_rows[j], ATT_BACK), lanes],
                                k_ref[0, pl.ds(diag_rows[j], ATT_Q), lanes]], axis=0)

    def window_values(j, lanes):
        chunks = []
        for i in range(back_chunks):
            rows = pl.multiple_of(back_rows[j] + i * ATT_Q, ATT_Q)
            chunk = v_ref[0, pl.ds(rows, ATT_Q), lanes]
            chunks.append(jnp.where(blks[j] > i, chunk, jnp.zeros_like(chunk)))
        chunks.append(v_ref[0, pl.ds(diag_rows[j], ATT_Q), lanes])
        return jnp.concatenate(chunks, axis=0)

    def window_softplus(z):
        sp = _softplus(z)
        return sp[:, 0:LANES], jnp.where(causal, sp[:, LANES:ATT_K0], 0.0), sp

    for j in range(ATT_BLOCKS):
        for p in range(ATT_PAIRS):
            lanes = slice(p * LANES, (p + 1) * LANES)
            z_ref[j * ATT_PAIRS + p] = _qk(stacked_q(j, p), window_keys(j, lanes))
    least = [[] for _ in range(ATT_BLOCKS)]
    for j in range(ATT_BLOCKS):
        tri = tri_ref[jnp.maximum(back_chunks - blks[j], 0)]
        for p in range(ATT_PAIRS):
            slot = j * ATT_PAIRS + p
            z = z_ref[slot]
            sp_lo, sp_hi, sp = window_softplus(z)
            z_ref[slot] = z - sp
            least[j].append(jnp.min(jnp.sum(sp_lo + sp_hi, axis=-1, keepdims=True)))
            sp_ref[slot, :, 0:LANES] = sp_lo.astype(BF16)
            sp_ref[slot, :, LANES:ATT_K0] = sp_hi.astype(BF16)
            after_ref[slot] = _dot(sp_ref[slot], tri)
    blocks = []
    for j in range(ATT_BLOCKS):
        accs = []
        for p in range(ATT_PAIRS):
            slot = j * ATT_PAIRS + p
            lanes = slice(p * LANES, (p + 1) * LANES)
            w = jnp.exp(z_ref[slot] + after_ref[slot])
            w = jnp.concatenate(
                [w[:, 0:LANES], jnp.where(causal, w[:, LANES:ATT_K0], 0.0)], axis=1)
            accs.append(weighted_values(w, window_values(j, lanes)))
        write_normalized(j, accs)
        blocks.append((blks[j], functools.reduce(jnp.minimum, least[j]) < -LOGW_FLOOR, accs))

    for j, (blk, more, accs) in enumerate(blocks):
        @pl.when(jnp.logical_and(blk > back_chunks, more))
        def _(j=j, blk=blk, accs=accs):
            carries = []
            for p in range(ATT_PAIRS):
                lanes = slice(p * LANES, (p + 1) * LANES)
                z = _qk(stacked_q(j, p), window_keys(j, lanes))
                sp_lo, sp_hi, _ = window_softplus(z)
                carries.append(-jnp.sum(sp_lo + sp_hi, axis=-1, keepdims=True))

            def cond(st):
                jb, more, _, _ = st
                return jnp.logical_and(jb >= 0, more > 0)

            def body(st):
                jb, _, carries, accs = st
                ks = pl.multiple_of(jb * ATT_KB, ATT_KB)
                new_c, new_a = [], []
                for p in range(ATT_PAIRS):
                    lanes = slice(p * LANES, (p + 1) * LANES)
                    kb = k_ref[0, pl.ds(ks, ATT_KB), lanes]
                    vb = v_ref[0, pl.ds(ks, ATT_KB), lanes]
                    z = _qk(stacked_q(j, p), kb)
                    sp = _softplus(z)
                    w = jnp.exp((z - sp) + _dot(sp.astype(BF16), tri_b) + carries[p])
                    new_a.append(accs[p] + weighted_values(w, vb))
                    new_c.append(carries[p] - jnp.sum(sp, axis=-1, keepdims=True))
                return jb - 1, unfinished(new_c), tuple(new_c), tuple(new_a)

            init = (blk - back_chunks - 1, jnp.int32(1), tuple(carries), tuple(accs))
            write_normalized(j, lax.while_loop(cond, body, init)[3])


def _attention(qkv, g_attn_out):
    bsz, seq, _ = qkv.shape
    tq = ATT_BLOCKS * ATT_Q
    slots = ATT_BLOCKS * ATT_PAIRS
    kv_spec = functools.partial(pl.BlockSpec, (1, seq, ATTN_WIDTH),
                                pipeline_mode=pl.Buffered(1))
    return pl.pallas_call(
        _attn_kernel,
        grid=(bsz, seq // tq),
        in_specs=[
            pl.BlockSpec((1, tq, ATTN_WIDTH), lambda b, i: (b, i, 0)),
            kv_spec(lambda b, i: (b, 0, 1)),
            kv_spec(lambda b, i: (b, 0, 2)),
            _const_spec((1, ATTN_WIDTH)),
        ],
        out_specs=pl.BlockSpec((1, tq, ATTN_WIDTH), lambda b, i: (b, i, 0)),
        out_shape=jax.ShapeDtypeStruct((bsz, seq, ATTN_WIDTH), BF16),
        scratch_shapes=[
            pltpu.VMEM((ATT_BACK // ATT_Q + 1, ATT_K0, ATT_K0), BF16),
            pltpu.VMEM((slots, 2 * ATT_Q, ATT_K0), F32),
            pltpu.VMEM((slots, 2 * ATT_Q, ATT_K0), BF16),
            pltpu.VMEM((slots, 2 * ATT_Q, ATT_K0), F32),
        ],
        compiler_params=pltpu.CompilerParams(
            dimension_semantics=("arbitrary", "arbitrary"),
            vmem_limit_bytes=VMEM_LIMIT),
        name="attn",
    )(qkv, qkv, qkv, g_attn_out.reshape(1, ATTN_WIDTH))


def _conv_branch_tasks(buf_ref, shift_ref, cw_ref, cb_ref, lng_ref, lnb_ref, act_ref):
    lead = HALO - (CONV_KERNEL - 1)
    groups = CONV_ROWS // SUBLANES

    def fill_shifts(r0):
        for s in range(1, SUBLANES):
            shift_ref[s - 1] = buf_ref[r0 + s:r0 + s + MIXOUT_SUB + SHIFT_SPAN, :]

    def chunk(r0, r):
        y = jnp.broadcast_to(cb_ref[...][None], (groups, SUBLANES, CONV_WIDTH_CH))
        for t in range(CONV_KERNEL):
            s = (lead + t) % SUBLANES
            base = r * CONV_ROWS + lead + t - s
            if s == 0:
                rows = buf_ref[r0 + base:r0 + base + CONV_ROWS, :]
            else:
                rows = shift_ref[s - 1, base:base + CONV_ROWS, :]
            y = y + cw_ref[t][None] * rows.reshape(groups, SUBLANES, CONV_WIDTH_CH)
        y = y.reshape(CONV_ROWS, CONV_WIDTH_CH)
        mu = jnp.mean(y, axis=-1, keepdims=True)
        yc = y - mu
        var = jnp.mean(yc * yc, axis=-1, keepdims=True)
        yn = yc * lax.rsqrt(var + LN_EPS) * lng_ref[...] + lnb_ref[...]
        act_ref[r0 + r * CONV_ROWS:r0 + (r + 1) * CONV_ROWS, :] = (
            yn * _sigmoid(yn)).astype(BF16)

    tasks = []
    for r0 in range(0, MIXOUT_TM, MIXOUT_SUB):
        tasks.append(functools.partial(fill_shifts, r0))
        for r in range(MIXOUT_SUB // CONV_ROWS):
            tasks.append(functools.partial(chunk, r0, r))
    return tasks


def _mixout_kernel(x_ref, mod_ref, a_ref, u_ref, halo_ref, cw_ref, cb_ref, lng_ref,
                   lnb_ref, w_ref, gpost_ref, o_ref, buf_ref, shift_ref, act_ref):
    i = pl.program_id(1)
    has_prev = (i > 0).astype(F32)
    buf_ref[0:HALO, :] = halo_ref[0] * has_prev
    buf_ref[HALO:HALO + MIXOUT_TM, :] = u_ref[0]
    for task in _conv_branch_tasks(buf_ref, shift_ref, cw_ref, cb_ref, lng_ref, lnb_ref,
                                   act_ref):
        task()
    gate = mod_ref[0, 5:6, :]
    for s in range(MIXOUT_TM // MIXOUT_SUB):
        rows = slice(s * MIXOUT_SUB, (s + 1) * MIXOUT_SUB)
        m = (_dot(a_ref[0, rows, :], w_ref[0:ATTN_WIDTH, :])
             + _dot(act_ref[rows, :], w_ref[ATTN_WIDTH:D_MODEL, :]))
        o_ref[0, rows, :] = x_ref[0, rows, :] + (1.0 + gate) * _rms(m, gpost_ref[...])


def _mix_out(x, mod, a, u, conv_w, conv_b, ln_g, ln_b, w_out_mix, g_post):
    bsz, seq, d = x.shape
    tm = MIXOUT_TM
    cw = CONV_WIDTH_CH
    per = tm // HALO
    return pl.pallas_call(
        _mixout_kernel,
        grid=(bsz, seq // tm),
        in_specs=[
            pl.BlockSpec((1, tm, d), lambda b, i: (b, i, 0)),
            pl.BlockSpec((1, 3 * N_SUBLAYERS, d), lambda b, i: (b, 0, 0)),
            pl.BlockSpec((1, tm, ATTN_WIDTH), lambda b, i: (b, i, 0)),
            pl.BlockSpec((1, tm, cw), lambda b, i: (b, i, 0)),
            pl.BlockSpec((1, HALO, cw), lambda b, i: (b, jnp.maximum(i * per - 1, 0), 0)),
            _const_spec((CONV_KERNEL, SUBLANES, cw)),
            _const_spec((SUBLANES, cw)),
            _const_spec((1, cw)),
            _const_spec((1, cw)),
            _const_spec((d, d)),
            _const_spec((1, d)),
        ],
        out_specs=pl.BlockSpec((1, tm, d), lambda b, i: (b, i, 0)),
        out_shape=jax.ShapeDtypeStruct(x.shape, F32),
        scratch_shapes=[
            pltpu.VMEM((HALO + tm, cw), F32),
            pltpu.VMEM((SUBLANES - 1, MIXOUT_SUB + SHIFT_SPAN, cw), F32),
            pltpu.VMEM((tm, cw), BF16),
        ],
        compiler_params=pltpu.CompilerParams(vmem_limit_bytes=VMEM_LIMIT),
        name="mix_out",
    )(x, mod, a, u, u,
      jnp.broadcast_to(conv_w[:, None, :], (CONV_KERNEL, SUBLANES, cw)),
      jnp.broadcast_to(conv_b[None, :], (SUBLANES, cw)), ln_g.reshape(1, cw),
      ln_b.reshape(1, cw), w_out_mix, g_post.reshape(1, d))


def kernel(x, c, w_ada, b_ada, g_pre_ff1, g_post_ff1, ff1_w_in, ff1_w_out, g_pre_mix, g_post_mix, w_in_mix, g_attn_out, conv_w, conv_b, conv_ln_g, conv_ln_b, w_out_mix, g_pre_ff2, g_post_ff2, ff2_w_in, ff2_w_out):
    mod = _adaln(c, w_ada, b_ada)
    h = _ffn(x, mod, g_pre_ff1, g_post_ff1, ff1_w_in.astype(BF16),
             ff1_w_out.astype(BF16), 0, 0.5)
    qkv, u = _mix_in(h, mod, g_pre_mix, w_in_mix.astype(BF16))
    a = _attention(qkv, g_attn_out)
    h = _mix_out(h, mod, a, u, conv_w, conv_b, conv_ln_g, conv_ln_b,
                 w_out_mix.astype(BF16), g_post_mix)
    h = _ffn(h, mod, g_pre_ff2, g_post_ff2, ff2_w_in.astype(BF16),
             ff2_w_out.astype(BF16), 2, 0.5)
    return h
```

```python
import functools

import jax
import jax.numpy as jnp
from jax import lax
from jax.experimental import pallas as pl
from jax.experimental.pallas import tpu as pltpu

D_MODEL = 1024
ATTN_WIDTH = D_MODEL // 2
HEAD_DIM = 64
N_HEADS = ATTN_WIDTH // HEAD_DIM
CONV_WIDTH_CH = D_MODEL - ATTN_WIDTH
CONV_KERNEL = 31
D_FF = 2816
N_SUBLAYERS = 3
MIX_IN = 3 * ATTN_WIDTH + 2 * CONV_WIDTH_CH
RMS_EPS = 1e-6
LN_EPS = 1e-5

LANES = 128
SUBLANES = 8
MXU_DIM = 256
VMEM_LIMIT = 56 * 1024 * 1024

FFN_TM = 1024
FFN_SUB = 256
FFN_CHUNK = D_FF // 2
MIX_TM = 1024
MIX_SUB = 256
MIXOUT_TM = 256
MIXOUT_SUB = 256
CONV_ROWS = 32
HALO = 32
SHIFT_SPAN = HALO - SUBLANES
ATT_Q = 64
ATT_BLOCKS = 8
ATT_PAIRS = ATTN_WIDTH // LANES
ATT_K0 = 256
ATT_BACK = ATT_K0 - ATT_Q
ATT_KB = ATT_Q
LOGW_FLOOR = -88.0
LOG2E = 1.4426950408889634

F32 = jnp.float32
BF16 = jnp.bfloat16


def _sigmoid(x):
    return 1.0 / (1.0 + jnp.exp(-x))


def _dot(a, b):
    return jnp.dot(a, b, preferred_element_type=F32)


def _rms(x, g):
    ms = jnp.mean(x * x, axis=-1, keepdims=True)
    return x * lax.rsqrt(ms + RMS_EPS) * g


def _adaln_kernel(c_ref, w_ref, b_ref, o_ref):
    c = c_ref[...]
    s = c * _sigmoid(c)
    o_ref[...] = _dot(s.astype(BF16), w_ref[...].astype(BF16)) + b_ref[...]


def _adaln(c, w_ada, b_ada):
    bsz = c.shape[0]
    n = w_ada.shape[1]
    tn = 1024
    c_pad = jnp.zeros((SUBLANES, D_MODEL), F32).at[:bsz].set(c)
    out = pl.pallas_call(
        _adaln_kernel,
        grid=(n // tn,),
        in_specs=[
            pl.BlockSpec((SUBLANES, D_MODEL), lambda j: (0, 0)),
            pl.BlockSpec((D_MODEL, tn), lambda j: (0, j)),
            pl.BlockSpec((1, tn), lambda j: (0, j)),
        ],
        out_specs=pl.BlockSpec((SUBLANES, tn), lambda j: (0, j)),
        out_shape=jax.ShapeDtypeStruct((SUBLANES, n), F32),
        name="adaln",
    )(c_pad, w_ada, b_ada.reshape(1, n))
    return out[:bsz].reshape(bsz, 3 * N_SUBLAYERS, D_MODEL)


def _modulated(x, mod_ref, sub, g_pre):
    shift = mod_ref[0, 3 * sub:3 * sub + 1, :]
    scale = mod_ref[0, 3 * sub + 1:3 * sub + 2, :]
    return _rms(x, g_pre) * (1.0 + scale) + shift


def _ffn_rows(x, mod_ref, gpre_ref, gpost_ref, win_ref, wout_ref, sub, res_w):
    hb = _modulated(x, mod_ref, sub, gpre_ref[...]).astype(BF16)
    f = None
    for c in range(D_FF // FFN_CHUNK):
        lo = c * FFN_CHUNK
        g = _dot(hb, win_ref[:, lo:lo + FFN_CHUNK])
        u = _dot(hb, win_ref[:, D_FF + lo:D_FF + lo + FFN_CHUNK])
        act = (g * _sigmoid(g) * u).astype(BF16)
        part = _dot(act, wout_ref[lo:lo + FFN_CHUNK, :])
        f = part if f is None else f + part
    y = _rms(f, gpost_ref[...])
    gate = mod_ref[0, 3 * sub + 2:3 * sub + 3, :]
    return x + res_w * (1.0 + gate) * y


def _ffn_kernel(x_ref, mod_ref, gpre_ref, gpost_ref, win_ref, wout_ref, o_ref,
                *, sub, res_w):
    for s in range(FFN_TM // FFN_SUB):
        rows = slice(s * FFN_SUB, (s + 1) * FFN_SUB)
        o_ref[0, rows, :] = _ffn_rows(x_ref[0, rows, :], mod_ref, gpre_ref, gpost_ref,
                                      win_ref, wout_ref, sub, res_w)


def _const_spec(shape):
    nd = len(shape)
    return pl.BlockSpec(shape, lambda *_: (0,) * nd, pipeline_mode=pl.Buffered(1))


def _ffn(x, mod, g_pre, g_post, w_in, w_out, sub, res_w):
    bsz, seq, d = x.shape
    tm = FFN_TM
    return pl.pallas_call(
        functools.partial(_ffn_kernel, sub=sub, res_w=res_w),
        grid=(bsz, seq // tm),
        in_specs=[
            pl.BlockSpec((1, tm, d), lambda b, i: (b, i, 0)),
            pl.BlockSpec((1, 3 * N_SUBLAYERS, d), lambda b, i: (b, 0, 0)),
            _const_spec((1, d)),
            _const_spec((1, d)),
            _const_spec((d, 2 * D_FF)),
            _const_spec((D_FF, d)),
        ],
        out_specs=pl.BlockSpec((1, tm, d), lambda b, i: (b, i, 0)),
        out_shape=jax.ShapeDtypeStruct(x.shape, F32),
        compiler_params=pltpu.CompilerParams(vmem_limit_bytes=VMEM_LIMIT),
        name=f"ffn{sub}",
    )(x, mod, g_pre.reshape(1, d), g_post.reshape(1, d), w_in, w_out)


def _mixin_kernel(x_ref, mod_ref, gpre_ref, w_ref, qkv_ref, u_ref):
    aw = ATTN_WIDTH
    for s in range(MIX_TM // MIX_SUB):
        rows = slice(s * MIX_SUB, (s + 1) * MIX_SUB)
        hb = _modulated(x_ref[0, rows, :], mod_ref, 1, gpre_ref[...]).astype(BF16)
        q = _dot(hb, w_ref[:, 0:aw]) * (HEAD_DIM ** -0.5)
        qkv_ref[0, rows, 0:aw] = q.astype(BF16)
        kv = _dot(hb, w_ref[:, aw:3 * aw])
        qkv_ref[0, rows, aw:3 * aw] = kv.astype(BF16)
        cv = _dot(hb, w_ref[:, 3 * aw:3 * aw + CONV_WIDTH_CH])
        cg = _dot(hb, w_ref[:, 3 * aw + CONV_WIDTH_CH:MIX_IN])
        u_ref[0, rows, :] = cv * _sigmoid(cg)


def _mix_in(x, mod, g_pre, w_in_mix):
    bsz, seq, d = x.shape
    tm = MIX_TM
    return pl.pallas_call(
        _mixin_kernel,
        grid=(bsz, seq // tm),
        in_specs=[
            pl.BlockSpec((1, tm, d), lambda b, i: (b, i, 0)),
            pl.BlockSpec((1, 3 * N_SUBLAYERS, d), lambda b, i: (b, 0, 0)),
            _const_spec((1, d)),
            _const_spec((d, MIX_IN)),
        ],
        out_specs=[
            pl.BlockSpec((1, tm, 3 * ATTN_WIDTH), lambda b, i: (b, i, 0)),
            pl.BlockSpec((1, tm, CONV_WIDTH_CH), lambda b, i: (b, i, 0)),
        ],
        out_shape=[
            jax.ShapeDtypeStruct((bsz, seq, 3 * ATTN_WIDTH), BF16),
            jax.ShapeDtypeStruct((bsz, seq, CONV_WIDTH_CH), F32),
        ],
        compiler_params=pltpu.CompilerParams(vmem_limit_bytes=VMEM_LIMIT),
        name="mix_in",
    )(x, mod, g_pre.reshape(1, d), w_in_mix)


def _softplus(z):
    t = jnp.exp2(jnp.abs(z) * (-LOG2E))
    return jnp.maximum(z, 0.0) + jnp.log(1.0 + t)


def _qk(qm, k):
    return lax.dot_general(qm, k, (((1,), (1,)), ((), ())),
                           preferred_element_type=F32)


def _attn_kernel(q_ref, k_ref, v_ref, g_ref, o_ref, tri_ref, z_ref, sp_ref, after_ref):
    b = pl.program_id(0)
    step = pl.program_id(1)
    back_chunks = ATT_BACK // ATT_Q

    @pl.when((b == 0) & (step == 0))
    def _():
        r = lax.broadcasted_iota(jnp.int32, (ATT_K0, ATT_K0), 0)
        c = lax.broadcasted_iota(jnp.int32, (ATT_K0, ATT_K0), 1)
        for missing in range(back_chunks + 1):
            dead = (r >= ATT_BACK - missing * ATT_Q) & (r < ATT_BACK)
            tri_ref[missing] = jnp.where((r > c) & jnp.logical_not(dead), -1.0, 0.0).astype(BF16)

    first = lax.broadcasted_iota(jnp.int32, (1, LANES), 1) < HEAD_DIM
    rows2 = 2 * ATT_Q
    row = lax.broadcasted_iota(jnp.int32, (rows2, LANES), 0)
    col = lax.broadcasted_iota(jnp.int32, (rows2, LANES), 1) + (ATT_K0 - LANES)
    causal = (col - ATT_BACK) < jnp.bitwise_and(row, ATT_Q - 1)
    tri_b = tri_ref[0, 0:ATT_KB, 0:ATT_KB]

    def stacked_q(j, p):
        qp = q_ref[0, j * ATT_Q:(j + 1) * ATT_Q, p * LANES:(p + 1) * LANES]
        zero = jnp.zeros_like(qp)
        return jnp.concatenate(
            [jnp.where(first, qp, zero), jnp.where(first, zero, qp)], axis=0)

    def weighted_values(w, v):
        wb = w.astype(BF16)
        return jnp.where(first, _dot(wb[0:ATT_Q], v), _dot(wb[ATT_Q:rows2], v))

    def unfinished(carries):
        worst = functools.reduce(jnp.maximum, carries)
        return (jnp.max(worst) > LOGW_FLOOR).astype(jnp.int32)

    def write_normalized(j, accs):
        for p in range(ATT_PAIRS):
            lanes = slice(p * LANES, (p + 1) * LANES)
            o = accs[p]
            sq = o * o
            s_all = jnp.sum(sq, axis=-1, keepdims=True)
            s_first = jnp.sum(jnp.where(first, sq, 0.0), axis=-1, keepdims=True)
            ms = jnp.where(first, s_first, s_all - s_first) * (1.0 / HEAD_DIM)
            o_ref[0, j * ATT_Q:(j + 1) * ATT_Q, lanes] = (
                o * lax.rsqrt(ms + RMS_EPS) * g_ref[:, lanes]).astype(BF16)

    blks = [step * ATT_BLOCKS + j for j in range(ATT_BLOCKS)]
    back_rows = [pl.multiple_of(jnp.maximum(blk * ATT_Q - ATT_BACK, 0), ATT_Q) for blk in blks]
    diag_rows = [pl.multiple_of(blk * ATT_Q, ATT_Q) for blk in blks]

    def window_keys(j, lanes):
        return jnp.concatenate([k_ref[0, pl.ds(back_rows[j], ATT_BACK), lanes],
                                k_ref[0, pl.ds(diag_rows[j], ATT_Q), lanes]], axis=0)

    def window_values(j, lanes):
        chunks = []
        for i in range(back_chunks):
            rows = pl.multiple_of(back_rows[j] + i * ATT_Q, ATT_Q)
            chunk = v_ref[0, pl.ds(rows, ATT_Q), lanes]
            chunks.append(jnp.where(blks[j] > i, chunk, jnp.zeros_like(chunk)))
        chunks.append(v_ref[0, pl.ds(diag_rows[j], ATT_Q), lanes])
        return jnp.concatenate(chunks, axis=0)

    def window_softplus(z):
        sp = _softplus(z)
        return sp[:, 0:LANES], jnp.where(causal, sp[:, LANES:ATT_K0], 0.0), sp

    for j in range(ATT_BLOCKS):
        for p in range(ATT_PAIRS):
            lanes = slice(p * LANES, (p + 1) * LANES)
            z_ref[j * ATT_PAIRS + p] = _qk(stacked_q(j, p), window_keys(j, lanes))
    least = [[] for _ in range(ATT_BLOCKS)]
    for j in range(ATT_BLOCKS):
        tri = tri_ref[jnp.maximum(back_chunks - blks[j], 0)]
        for p in range(ATT_PAIRS):
            slot = j * ATT_PAIRS + p
            z = z_ref[slot]
            sp_lo, sp_hi, sp = window_softplus(z)
            z_ref[slot] = z - sp
            least[j].append(jnp.min(jnp.sum(sp_lo + sp_hi, axis=-1, keepdims=True)))
            sp_ref[slot, :, 0:LANES] = sp_lo.astype(BF16)
            sp_ref[slot, :, LANES:ATT_K0] = sp_hi.astype(BF16)
            after_ref[slot] = _dot(sp_ref[slot], tri)
    blocks = []
    for j in range(ATT_BLOCKS):
        accs = []
        for p in range(ATT_PAIRS):
            slot = j * ATT_PAIRS + p
            lanes = slice(p * LANES, (p + 1) * LANES)
            w = jnp.exp(z_ref[slot] + after_ref[slot])
            w = jnp.concatenate(
                [w[:, 0:LANES], jnp.where(causal, w[:, LANES:ATT_K0], 0.0)], axis=1)
            accs.append(weighted_values(w, window_values(j, lanes)))
        write_normalized(j, accs)
        blocks.append((blks[j], functools.reduce(jnp.minimum, least[j]) < -LOGW_FLOOR, accs))

    for j, (blk, more, accs) in enumerate(blocks):
        @pl.when(jnp.logical_and(blk > back_chunks, more))
        def _(j=j, blk=blk, accs=accs):
            carries = []
            for p in range(ATT_PAIRS):
                lanes = slice(p * LANES, (p + 1) * LANES)
                z = _qk(stacked_q(j, p), window_keys(j, lanes))
                sp_lo, sp_hi, _ = window_softplus(z)
                carries.append(-jnp.sum(sp_lo + sp_hi, axis=-1, keepdims=True))

            def cond(st):
                jb, more, _, _ = st
                return jnp.logical_and(jb >= 0, more > 0)

            def body(st):
                jb, _, carries, accs = st
                ks = pl.multiple_of(jb * ATT_KB, ATT_KB)
                new_c, new_a = [], []
                for p in range(ATT_PAIRS):
                    lanes = slice(p * LANES, (p + 1) * LANES)
                    kb = k_ref[0, pl.ds(ks, ATT_KB), lanes]
                    vb = v_ref[0, pl.ds(ks, ATT_KB), lanes]
                    z = _qk(stacked_q(j, p), kb)
                    sp = _softplus(z)
                    w = jnp.exp((z - sp) + _dot(sp.astype(BF16), tri_b) + carries[p])
                    new_a.append(accs[p] + weighted_values(w, vb))
                    new_c.append(carries[p] - jnp.sum(sp, axis=-1, keepdims=True))
                return jb - 1, unfinished(new_c), tuple(new_c), tuple(new_a)

            init = (blk - back_chunks - 1, jnp.int32(1), tuple(carries), tuple(accs))
            write_normalized(j, lax.while_loop(cond, body, init)[3])


def _attention(qkv, g_attn_out):
    bsz, seq, _ = qkv.shape
    tq = ATT_BLOCKS * ATT_Q
    slots = ATT_BLOCKS * ATT_PAIRS
    kv_spec = functools.partial(pl.BlockSpec, (1, seq, ATTN_WIDTH))
    return pl.pallas_call(
        _attn_kernel,
        grid=(bsz, seq // tq),
        in_specs=[
            pl.BlockSpec((1, tq, ATTN_WIDTH), lambda b, i: (b, i, 0)),
            kv_spec(lambda b, i: (b, 0, 1)),
            kv_spec(lambda b, i: (b, 0, 2)),
            _const_spec((1, ATTN_WIDTH)),
        ],
        out_specs=pl.BlockSpec((1, tq, ATTN_WIDTH), lambda b, i: (b, i, 0)),
        out_shape=jax.ShapeDtypeStruct((bsz, seq, ATTN_WIDTH), BF16),
        scratch_shapes=[
            pltpu.VMEM((ATT_BACK // ATT_Q + 1, ATT_K0, ATT_K0), BF16),
            pltpu.VMEM((slots, 2 * ATT_Q, ATT_K0), F32),
            pltpu.VMEM((slots, 2 * ATT_Q, ATT_K0), BF16),
            pltpu.VMEM((slots, 2 * ATT_Q, ATT_K0), F32),
        ],
        compiler_params=pltpu.CompilerParams(
            dimension_semantics=("arbitrary", "arbitrary"),
            vmem_limit_bytes=VMEM_LIMIT),
        name="attn",
    )(qkv, qkv, qkv, g_attn_out.reshape(1, ATTN_WIDTH))


def _conv_branch_tasks(buf_ref, shift_ref, cw_ref, cb_ref, lng_ref, lnb_ref, act_ref):
    lead = HALO - (CONV_KERNEL - 1)
    groups = CONV_ROWS // SUBLANES

    def fill_shifts(r0):
        for s in range(1, SUBLANES):
            shift_ref[s - 1] = buf_ref[r0 + s:r0 + s + MIXOUT_SUB + SHIFT_SPAN, :]

    def chunk(r0, r):
        y = jnp.broadcast_to(cb_ref[...][None], (groups, SUBLANES, CONV_WIDTH_CH))
        for t in range(CONV_KERNEL):
            s = (lead + t) % SUBLANES
            base = r * CONV_ROWS + lead + t - s
            if s == 0:
                rows = buf_ref[r0 + base:r0 + base + CONV_ROWS, :]
            else:
                rows = shift_ref[s - 1, base:base + CONV_ROWS, :]
            y = y + cw_ref[t][None] * rows.reshape(groups, SUBLANES, CONV_WIDTH_CH)
        y = y.reshape(CONV_ROWS, CONV_WIDTH_CH)
        mu = jnp.mean(y, axis=-1, keepdims=True)
        yc = y - mu
        var = jnp.mean(yc * yc, axis=-1, keepdims=True)
        yn = yc * lax.rsqrt(var + LN_EPS) * lng_ref[...] + lnb_ref[...]
        act_ref[r0 + r * CONV_ROWS:r0 + (r + 1) * CONV_ROWS, :] = (
            yn * _sigmoid(yn)).astype(BF16)

    tasks = []
    for r0 in range(0, MIXOUT_TM, MIXOUT_SUB):
        tasks.append(functools.partial(fill_shifts, r0))
        for r in range(MIXOUT_SUB // CONV_ROWS):
            tasks.append(functools.partial(chunk, r0, r))
    return tasks


def _mixout_kernel(x_ref, mod_ref, a_ref, u_ref, halo_ref, cw_ref, cb_ref, lng_ref,
                   lnb_ref, w_ref, gpost_ref, o_ref, buf_ref, shift_ref, act_ref):
    i = pl.program_id(1)
    has_prev = (i > 0).astype(F32)
    buf_ref[0:HALO, :] = halo_ref[0] * has_prev
    buf_ref[HALO:HALO + MIXOUT_TM, :] = u_ref[0]
    for task in _conv_branch_tasks(buf_ref, shift_ref, cw_ref, cb_ref, lng_ref, lnb_ref,
                                   act_ref):
        task()
    gate = mod_ref[0, 5:6, :]
    for s in range(MIXOUT_TM // MIXOUT_SUB):
        rows = slice(s * MIXOUT_SUB, (s + 1) * MIXOUT_SUB)
        m = (_dot(a_ref[0, rows, :], w_ref[0:ATTN_WIDTH, :])
             + _dot(act_ref[rows, :], w_ref[ATTN_WIDTH:D_MODEL, :]))
        o_ref[0, rows, :] = x_ref[0, rows, :] + (1.0 + gate) * _rms(m, gpost_ref[...])


def _mix_out(x, mod, a, u, conv_w, conv_b, ln_g, ln_b, w_out_mix, g_post):
    bsz, seq, d = x.shape
    tm = MIXOUT_TM
    cw = CONV_WIDTH_CH
    per = tm // HALO
    return pl.pallas_call(
        _mixout_kernel,
        grid=(bsz, seq // tm),
        in_specs=[
            pl.BlockSpec((1, tm, d), lambda b, i: (b, i, 0)),
            pl.BlockSpec((1, 3 * N_SUBLAYERS, d), lambda b, i: (b, 0, 0)),
            pl.BlockSpec((1, tm, ATTN_WIDTH), lambda b, i: (b, i, 0)),
            pl.BlockSpec((1, tm, cw), lambda b, i: (b, i, 0)),
            pl.BlockSpec((1, HALO, cw), lambda b, i: (b, jnp.maximum(i * per - 1, 0), 0)),
            _const_spec((CONV_KERNEL, SUBLANES, cw)),
            _const_spec((SUBLANES, cw)),
            _const_spec((1, cw)),
            _const_spec((1, cw)),
            _const_spec((d, d)),
            _const_spec((1, d)),
        ],
        out_specs=pl.BlockSpec((1, tm, d), lambda b, i: (b, i, 0)),
        out_shape=jax.ShapeDtypeStruct(x.shape, F32),
        scratch_shapes=[
            pltpu.VMEM((HALO + tm, cw), F32),
            pltpu.VMEM((SUBLANES - 1, MIXOUT_SUB + SHIFT_SPAN, cw), F32),
            pltpu.VMEM((tm, cw), BF16),
        ],
        compiler_params=pltpu.CompilerParams(vmem_limit_bytes=VMEM_LIMIT),
        name="mix_out",
    )(x, mod, a, u, u,
      jnp.broadcast_to(conv_w[:, None, :], (CONV_KERNEL, SUBLANES, cw)),
      jnp.broadcast_to(conv_b[None, :], (SUBLANES, cw)), ln_g.reshape(1, cw),
      ln_b.reshape(1, cw), w_out_mix, g_post.reshape(1, d))


def kernel(x, c, w_ada, b_ada, g_pre_ff1, g_post_ff1, ff1_w_in, ff1_w_out, g_pre_mix, g_post_mix, w_in_mix, g_attn_out, conv_w, conv_b, conv_ln_g, conv_ln_b, w_out_mix, g_pre_ff2, g_post_ff2, ff2_w_in, ff2_w_out):
    mod = _adaln(c, w_ada, b_ada)
    h = _ffn(x, mod, g_pre_ff1, g_post_ff1, ff1_w_in.astype(BF16),
             ff1_w_out.astype(BF16), 0, 0.5)
    qkv, u = _mix_in(h, mod, g_pre_mix, w_in_mix.astype(BF16))
    a = _attention(qkv, g_attn_out)
    h = _mix_out(h, mod, a, u, conv_w, conv_b, conv_ln_g, conv_ln_b,
                 w_out_mix.astype(BF16), g_post_mix)
    h = _ffn(h, mod, g_pre_ff2, g_post_ff2, ff2_w_in.astype(BF16),
             ff2_w_out.astype(BF16), 2, 0.5)
    return h
```

```python
import functools

import jax
import jax.numpy as jnp
from jax import lax
from jax.experimental import pallas as pl
from jax.experimental.pallas import tpu as pltpu

D_MODEL = 1024
ATTN_WIDTH = D_MODEL // 2
HEAD_DIM = 64
N_HEADS = ATTN_WIDTH // HEAD_DIM
CONV_WIDTH_CH = D_MODEL - ATTN_WIDTH
CONV_KERNEL = 31
D_FF = 2816
N_SUBLAYERS = 3
MIX_IN = 3 * ATTN_WIDTH + 2 * CONV_WIDTH_CH
RMS_EPS = 1e-6
LN_EPS = 1e-5

LANES = 128
SUBLANES = 8
MXU_DIM = 256
VMEM_LIMIT = 56 * 1024 * 1024

FFN_TM = 1024
FFN_SUB = 256
FFN_CHUNK = D_FF // 2
MIX_TM = 1024
MIX_SUB = 256
MIXOUT_TM = 256
MIXOUT_SUB = 256
CONV_ROWS = 32
HALO = 32
SHIFT_SPAN = HALO - SUBLANES
ATT_Q = 64
ATT_BLOCKS = 8
ATT_PAIRS = ATTN_WIDTH // LANES
ATT_K0 = 256
ATT_BACK = ATT_K0 - ATT_Q
ATT_KB = ATT_Q
LOGW_FLOOR = -88.0
LOG2E = 1.4426950408889634

F32 = jnp.float32
BF16 = jnp.bfloat16


def _sigmoid(x):
    return 1.0 / (1.0 + jnp.exp(-x))


def _dot(a, b):
    return jnp.dot(a, b, preferred_element_type=F32)


def _rms(x, g):
    ms = jnp.mean(x * x, axis=-1, keepdims=True)
    return x * lax.rsqrt(ms + RMS_EPS) * g


def _adaln_kernel(c_ref, w_ref, b_ref, o_ref):
    c = c_ref[...]
    s = c * _sigmoid(c)
    o_ref[...] = _dot(s.astype(BF16), w_ref[...].astype(BF16)) + b_ref[...]


def _adaln(c, w_ada, b_ada):
    bsz = c.shape[0]
    n = w_ada.shape[1]
    tn = 1024
    c_pad = jnp.zeros((SUBLANES, D_MODEL), F32).at[:bsz].set(c)
    out = pl.pallas_call(
        _adaln_kernel,
        grid=(n // tn,),
        in_specs=[
            pl.BlockSpec((SUBLANES, D_MODEL), lambda j: (0, 0)),
            pl.BlockSpec((D_MODEL, tn), lambda j: (0, j)),
            pl.BlockSpec((1, tn), lambda j: (0, j)),
        ],
        out_specs=pl.BlockSpec((SUBLANES, tn), lambda j: (0, j)),
        out_shape=jax.ShapeDtypeStruct((SUBLANES, n), F32),
        name="adaln",
    )(c_pad, w_ada, b_ada.reshape(1, n))
    return out[:bsz].reshape(bsz, 3 * N_SUBLAYERS, D_MODEL)


def _modulated(x, mod_ref, sub, g_pre):
    shift = mod_ref[0, 3 * sub:3 * sub + 1, :]
    scale = mod_ref[0, 3 * sub + 1:3 * sub + 2, :]
    return _rms(x, g_pre) * (1.0 + scale) + shift


def _ffn_rows(x, mod_ref, gpre_ref, gpost_ref, win_ref, wout_ref, sub, res_w):
    hb = _modulated(x, mod_ref, sub, gpre_ref[...]).astype(BF16)
    f = None
    for c in range(D_FF // FFN_CHUNK):
        lo = c * FFN_CHUNK
        g = _dot(hb, win_ref[:, lo:lo + FFN_CHUNK])
        u = _dot(hb, win_ref[:, D_FF + lo:D_FF + lo + FFN_CHUNK])
        act = (g * _sigmoid(g) * u).astype(BF16)
        part = _dot(act, wout_ref[lo:lo + FFN_CHUNK, :])
        f = part if f is None else f + part
    y = _rms(f, gpost_ref[...])
    gate = mod_ref[0, 3 * sub + 2:3 * sub + 3, :]
    return x + res_w * (1.0 + gate) * y


def _ffn_kernel(x_ref, mod_ref, gpre_ref, gpost_ref, win_ref, wout_ref, *rest,
                sub, res_w):
    n_cast = len(rest) // 2
    o_ref = rest[n_cast]
    for s in range(FFN_TM // FFN_SUB):
        rows = slice(s * FFN_SUB, (s + 1) * FFN_SUB)
        o_ref[0, rows, :] = _ffn_rows(x_ref[0, rows, :], mod_ref, gpre_ref, gpost_ref,
                                      win_ref, wout_ref, sub, res_w)
    for src_ref, dst_ref in zip(rest[:n_cast], rest[n_cast + 1:]):
        dst_ref[...] = src_ref[...].astype(BF16)


def _const_spec(shape):
    nd = len(shape)
    return pl.BlockSpec(shape, lambda *_: (0,) * nd, pipeline_mode=pl.Buffered(1))


def _row_blocks(rows, steps):
    bf16_rows = 2 * SUBLANES
    return max(n for n in range(1, steps + 1)
               if rows % n == 0 and (rows // n) % bf16_rows == 0)


def _ffn(x, mod, g_pre, g_post, w_in, w_out, sub, res_w, cast_along=()):
    bsz, seq, d = x.shape
    tm = FFN_TM
    n_i = seq // tm
    cast_specs, cast_shapes = [], []
    for w in cast_along:
        n_blk = _row_blocks(w.shape[0], bsz * n_i)
        spec = pl.BlockSpec(
            (w.shape[0] // n_blk, w.shape[1]),
            lambda b, i, n_blk=n_blk: (jnp.minimum(b * n_i + i, n_blk - 1), 0))
        cast_specs.append(spec)
        cast_shapes.append(jax.ShapeDtypeStruct(w.shape, BF16))
    out = pl.pallas_call(
        functools.partial(_ffn_kernel, sub=sub, res_w=res_w),
        grid=(bsz, n_i),
        in_specs=[
            pl.BlockSpec((1, tm, d), lambda b, i: (b, i, 0)),
            pl.BlockSpec((1, 3 * N_SUBLAYERS, d), lambda b, i: (b, 0, 0)),
            _const_spec((1, d)),
            _const_spec((1, d)),
            _const_spec((d, 2 * D_FF)),
            _const_spec((D_FF, d)),
        ] + cast_specs,
        out_specs=[pl.BlockSpec((1, tm, d), lambda b, i: (b, i, 0))] + cast_specs,
        out_shape=[jax.ShapeDtypeStruct(x.shape, F32)] + cast_shapes,
        compiler_params=pltpu.CompilerParams(
            dimension_semantics=("arbitrary", "arbitrary"),
            vmem_limit_bytes=VMEM_LIMIT),
        name=f"ffn{sub}",
    )(x, mod, g_pre.reshape(1, d), g_post.reshape(1, d), w_in, w_out, *cast_along)
    return out[0], out[1:]


def _mixin_kernel(x_ref, mod_ref, gpre_ref, w_ref, qkv_ref, u_ref):
    aw = ATTN_WIDTH
    for s in range(MIX_TM // MIX_SUB):
        rows = slice(s * MIX_SUB, (s + 1) * MIX_SUB)
        hb = _modulated(x_ref[0, rows, :], mod_ref, 1, gpre_ref[...]).astype(BF16)
        q = _dot(hb, w_ref[:, 0:aw]) * (HEAD_DIM ** -0.5)
        qkv_ref[0, rows, 0:aw] = q.astype(BF16)
        kv = _dot(hb, w_ref[:, aw:3 * aw])
        qkv_ref[0, rows, aw:3 * aw] = kv.astype(BF16)
        cv = _dot(hb, w_ref[:, 3 * aw:3 * aw + CONV_WIDTH_CH])
        cg = _dot(hb, w_ref[:, 3 * aw + CONV_WIDTH_CH:MIX_IN])
        u_ref[0, rows, :] = cv * _sigmoid(cg)


def _mix_in(x, mod, g_pre, w_in_mix):
    bsz, seq, d = x.shape
    tm = MIX_TM
    return pl.pallas_call(
        _mixin_kernel,
        grid=(bsz, seq // tm),
        in_specs=[
            pl.BlockSpec((1, tm, d), lambda b, i: (b, i, 0)),
            pl.BlockSpec((1, 3 * N_SUBLAYERS, d), lambda b, i: (b, 0, 0)),
            _const_spec((1, d)),
            _const_spec((d, MIX_IN)),
        ],
        out_specs=[
            pl.BlockSpec((1, tm, 3 * ATTN_WIDTH), lambda b, i: (b, i, 0)),
            pl.BlockSpec((1, tm, CONV_WIDTH_CH), lambda b, i: (b, i, 0)),
        ],
        out_shape=[
            jax.ShapeDtypeStruct((bsz, seq, 3 * ATTN_WIDTH), BF16),
            jax.ShapeDtypeStruct((bsz, seq, CONV_WIDTH_CH), F32),
        ],
        compiler_params=pltpu.CompilerParams(vmem_limit_bytes=VMEM_LIMIT),
        name="mix_in",
    )(x, mod, g_pre.reshape(1, d), w_in_mix)


def _softplus(z):
    t = jnp.exp2(jnp.abs(z) * (-LOG2E))
    return jnp.maximum(z, 0.0) + jnp.log(1.0 + t)


def _qk(qm, k):
    return lax.dot_general(qm, k, (((1,), (1,)), ((), ())),
                           preferred_element_type=F32)


def _attn_kernel(q_ref, k_ref, v_ref, g_ref, o_ref, tri_ref, z_ref, sp_ref, after_ref):
    b = pl.program_id(0)
    step = pl.program_id(1)
    back_chunks = ATT_BACK // ATT_Q

    @pl.when((b == 0) & (step == 0))
    def _():
        r = lax.broadcasted_iota(jnp.int32, (ATT_K0, ATT_K0), 0)
        c = lax.broadcasted_iota(jnp.int32, (ATT_K0, ATT_K0), 1)
        for missing in range(back_chunks + 1):
            dead = (r >= ATT_BACK - missing * ATT_Q) & (r < ATT_BACK)
            tri_ref[missing] = jnp.where((r > c) & jnp.logical_not(dead), -1.0, 0.0).astype(BF16)

    first = lax.broadcasted_iota(jnp.int32, (1, LANES), 1) < HEAD_DIM
    rows2 = 2 * ATT_Q
    row = lax.broadcasted_iota(jnp.int32, (rows2, LANES), 0)
    col = lax.broadcasted_iota(jnp.int32, (rows2, LANES), 1) + (ATT_K0 - LANES)
    causal = (col - ATT_BACK) < jnp.bitwise_and(row, ATT_Q - 1)
    tri_b = tri_ref[0, 0:ATT_KB, 0:ATT_KB]

    def stacked_q(j, p):
        qp = q_ref[0, j * ATT_Q:(j + 1) * ATT_Q, p * LANES:(p + 1) * LANES]
        zero = jnp.zeros_like(qp)
        return jnp.concatenate(
            [jnp.where(first, qp, zero), jnp.where(first, zero, qp)], axis=0)

    def weighted_values(w, v):
        wb = w.astype(BF16)
        return jnp.where(first, _dot(wb[0:ATT_Q], v), _dot(wb[ATT_Q:rows2], v))

    def unfinished(carries):
        worst = functools.reduce(jnp.maximum, carries)
        return (jnp.max(worst) > LOGW_FLOOR).astype(jnp.int32)

    def write_normalized(j, accs):
        for p in range(ATT_PAIRS):
            lanes = slice(p * LANES, (p + 1) * LANES)
            o = accs[p]
            sq = o * o
            s_all = jnp.sum(sq, axis=-1, keepdims=True)
            s_first = jnp.sum(jnp.where(first, sq, 0.0), axis=-1, keepdims=True)
            ms = jnp.where(first, s_first, s_all - s_first) * (1.0 / HEAD_DIM)
            o_ref[0, j * ATT_Q:(j + 1) * ATT_Q, lanes] = (
                o * lax.rsqrt(ms + RMS_EPS) * g_ref[:, lanes]).astype(BF16)

    blks = [step * ATT_BLOCKS + j for j in range(ATT_BLOCKS)]
    back_rows = [pl.multiple_of(jnp.maximum(blk * ATT_Q - ATT_BACK, 0), ATT_Q) for blk in blks]
    diag_rows = [pl.multiple_of(blk * ATT_Q, ATT_Q) for blk in blks]

    def window_keys(j, lanes):
        return jnp.concatenate([k_ref[0, pl.ds(back_rows[j], ATT_BACK), lanes],
                                k_ref[0, pl.ds(diag_rows[j], ATT_Q), lanes]], axis=0)

    def window_values(j, lanes):
        chunks = []
        for i in range(back_chunks):
            rows = pl.multiple_of(back_rows[j] + i * ATT_Q, ATT_Q)
            chunk = v_ref[0, pl.ds(rows, ATT_Q), lanes]
            chunks.append(jnp.where(blks[j] > i, chunk, jnp.zeros_like(chunk)))
        chunks.append(v_ref[0, pl.ds(diag_rows[j], ATT_Q), lanes])
        return jnp.concatenate(chunks, axis=0)

    def window_softplus(z):
        sp = _softplus(z)
        return sp[:, 0:LANES], jnp.where(causal, sp[:, LANES:ATT_K0], 0.0), sp

    for j in range(ATT_BLOCKS):
        for p in range(ATT_PAIRS):
            lanes = slice(p * LANES, (p + 1) * LANES)
            z_ref[j * ATT_PAIRS + p] = _qk(stacked_q(j, p), window_keys(j, lanes))
    least = [[] for _ in range(ATT_BLOCKS)]
    for j in range(ATT_BLOCKS):
        tri = tri_ref[jnp.maximum(back_chunks - blks[j], 0)]
        for p in range(ATT_PAIRS):
            slot = j * ATT_PAIRS + p
            z = z_ref[slot]
            sp_lo, sp_hi, sp = window_softplus(z)
            z_ref[slot] = z - sp
            least[j].append(jnp.min(jnp.sum(sp_lo + sp_hi, axis=-1, keepdims=True)))
            sp_ref[slot, :, 0:LANES] = sp_lo.astype(BF16)
            sp_ref[slot, :, LANES:ATT_K0] = sp_hi.astype(BF16)
            after_ref[slot] = _dot(sp_ref[slot], tri)
    blocks = []
    for j in range(ATT_BLOCKS):
        accs = []
        for p in range(ATT_PAIRS):
            slot = j * ATT_PAIRS + p
            lanes = slice(p * LANES, (p + 1) * LANES)
            w = jnp.exp(z_ref[slot] + after_ref[slot])
            w = jnp.concatenate(
                [w[:, 0:LANES], jnp.where(causal, w[:, LANES:ATT_K0], 0.0)], axis=1)
            accs.append(weighted_values(w, window_values(j, lanes)))
        write_normalized(j, accs)
        blocks.append((blks[j], functools.reduce(jnp.minimum, least[j]) < -LOGW_FLOOR, accs))

    for j, (blk, more, accs) in enumerate(blocks):
        @pl.when(jnp.logical_and(blk > back_chunks, more))
        def _(j=j, blk=blk, accs=accs):
            carries = []
            for p in range(ATT_PAIRS):
                lanes = slice(p * LANES, (p + 1) * LANES)
                z = _qk(stacked_q(j, p), window_keys(j, lanes))
                sp_lo, sp_hi, _ = window_softplus(z)
                carries.append(-jnp.sum(sp_lo + sp_hi, axis=-1, keepdims=True))

            def cond(st):
                jb, more, _, _ = st
                return jnp.logical_and(jb >= 0, more > 0)

            def body(st):
                jb, _, carries, accs = st
                ks = pl.multiple_of(jb * ATT_KB, ATT_KB)
                new_c, new_a = [], []
                for p in range(ATT_PAIRS):
                    lanes = slice(p * LANES, (p + 1) * LANES)
                    kb = k_ref[0, pl.ds(ks, ATT_KB), lanes]
                    vb = v_ref[0, pl.ds(ks, ATT_KB), lanes]
                    z = _qk(stacked_q(j, p), kb)
                    sp = _softplus(z)
                    w = jnp.exp((z - sp) + _dot(sp.astype(BF16), tri_b) + carries[p])
                    new_a.append(accs[p] + weighted_values(w, vb))
                    new_c.append(carries[p] - jnp.sum(sp, axis=-1, keepdims=True))
                return jb - 1, unfinished(new_c), tuple(new_c), tuple(new_a)

            init = (blk - back_chunks - 1, jnp.int32(1), tuple(carries), tuple(accs))
            write_normalized(j, lax.while_loop(cond, body, init)[3])


def _attention(qkv, g_attn_out):
    bsz, seq, _ = qkv.shape
    tq = ATT_BLOCKS * ATT_Q
    slots = ATT_BLOCKS * ATT_PAIRS
    kv_spec = functools.partial(pl.BlockSpec, (1, seq, ATTN_WIDTH))
    return pl.pallas_call(
        _attn_kernel,
        grid=(bsz, seq // tq),
        in_specs=[
            pl.BlockSpec((1, tq, ATTN_WIDTH), lambda b, i: (b, i, 0)),
            kv_spec(lambda b, i: (b, 0, 1)),
            kv_spec(lambda b, i: (b, 0, 2)),
            _const_spec((1, ATTN_WIDTH)),
        ],
        out_specs=pl.BlockSpec((1, tq, ATTN_WIDTH), lambda b, i: (b, i, 0)),
        out_shape=jax.ShapeDtypeStruct((bsz, seq, ATTN_WIDTH), BF16),
        scratch_shapes=[
            pltpu.VMEM((ATT_BACK // ATT_Q + 1, ATT_K0, ATT_K0), BF16),
            pltpu.VMEM((slots, 2 * ATT_Q, ATT_K0), F32),
            pltpu.VMEM((slots, 2 * ATT_Q, ATT_K0), BF16),
            pltpu.VMEM((slots, 2 * ATT_Q, ATT_K0), F32),
        ],
        compiler_params=pltpu.CompilerParams(
            dimension_semantics=("arbitrary", "arbitrary"),
            vmem_limit_bytes=VMEM_LIMIT),
        name="attn",
    )(qkv, qkv, qkv, g_attn_out.reshape(1, ATTN_WIDTH))


def _conv_branch_tasks(buf_ref, shift_ref, cw_ref, cb_ref, lng_ref, lnb_ref, act_ref):
    lead = HALO - (CONV_KERNEL - 1)
    groups = CONV_ROWS // SUBLANES

    def fill_shifts(r0):
        for s in range(1, SUBLANES):
            shift_ref[s - 1] = buf_ref[r0 + s:r0 + s + MIXOUT_SUB + SHIFT_SPAN, :]

    def chunk(r0, r):
        y = jnp.broadcast_to(cb_ref[...][None], (groups, SUBLANES, CONV_WIDTH_CH))
        for t in range(CONV_KERNEL):
            s = (lead + t) % SUBLANES
            base = r * CONV_ROWS + lead + t - s
            if s == 0:
                rows = buf_ref[r0 + base:r0 + base + CONV_ROWS, :]
            else:
                rows = shift_ref[s - 1, base:base + CONV_ROWS, :]
            y = y + cw_ref[t][None] * rows.reshape(groups, SUBLANES, CONV_WIDTH_CH)
        y = y.reshape(CONV_ROWS, CONV_WIDTH_CH)
        mu = jnp.mean(y, axis=-1, keepdims=True)
        yc = y - mu
        var = jnp.mean(yc * yc, axis=-1, keepdims=True)
        yn = yc * lax.rsqrt(var + LN_EPS) * lng_ref[...] + lnb_ref[...]
        act_ref[r0 + r * CONV_ROWS:r0 + (r + 1) * CONV_ROWS, :] = (
            yn * _sigmoid(yn)).astype(BF16)

    tasks = []
    for r0 in range(0, MIXOUT_TM, MIXOUT_SUB):
        tasks.append(functools.partial(fill_shifts, r0))
        for r in range(MIXOUT_SUB // CONV_ROWS):
            tasks.append(functools.partial(chunk, r0, r))
    return tasks


def _mixout_kernel(x_ref, mod_ref, a_ref, u_ref, halo_ref, cw_ref, cb_ref, lng_ref,
                   lnb_ref, w_ref, gpost_ref, o_ref, buf_ref, shift_ref, act_ref):
    i = pl.program_id(1)
    has_prev = (i > 0).astype(F32)
    buf_ref[0:HALO, :] = halo_ref[0] * has_prev
    buf_ref[HALO:HALO + MIXOUT_TM, :] = u_ref[0]
    for task in _conv_branch_tasks(buf_ref, shift_ref, cw_ref, cb_ref, lng_ref, lnb_ref,
                                   act_ref):
        task()
    gate = mod_ref[0, 5:6, :]
    for s in range(MIXOUT_TM // MIXOUT_SUB):
        rows = slice(s * MIXOUT_SUB, (s + 1) * MIXOUT_SUB)
        m = (_dot(a_ref[0, rows, :], w_ref[0:ATTN_WIDTH, :])
             + _dot(act_ref[rows, :], w_ref[ATTN_WIDTH:D_MODEL, :]))
        o_ref[0, rows, :] = x_ref[0, rows, :] + (1.0 + gate) * _rms(m, gpost_ref[...])


def _mix_out(x, mod, a, u, conv_w, conv_b, ln_g, ln_b, w_out_mix, g_post):
    bsz, seq, d = x.shape
    tm = MIXOUT_TM
    cw = CONV_WIDTH_CH
    per = tm // HALO
    return pl.pallas_call(
        _mixout_kernel,
        grid=(bsz, seq // tm),
        in_specs=[
            pl.BlockSpec((1, tm, d), lambda b, i: (b, i, 0)),
            pl.BlockSpec((1, 3 * N_SUBLAYERS, d), lambda b, i: (b, 0, 0)),
            pl.BlockSpec((1, tm, ATTN_WIDTH), lambda b, i: (b, i, 0)),
            pl.BlockSpec((1, tm, cw), lambda b, i: (b, i, 0)),
            pl.BlockSpec((1, HALO, cw), lambda b, i: (b, jnp.maximum(i * per - 1, 0), 0)),
            _const_spec((CONV_KERNEL, SUBLANES, cw)),
            _const_spec((SUBLANES, cw)),
            _const_spec((1, cw)),
            _const_spec((1, cw)),
            _const_spec((d, d)),
            _const_spec((1, d)),
        ],
        out_specs=pl.BlockSpec((1, tm, d), lambda b, i: (b, i, 0)),
        out_shape=jax.ShapeDtypeStruct(x.shape, F32),
        scratch_shapes=[
            pltpu.VMEM((HALO + tm, cw), F32),
            pltpu.VMEM((SUBLANES - 1, MIXOUT_SUB + SHIFT_SPAN, cw), F32),
            pltpu.VMEM((tm, cw), BF16),
        ],
        compiler_params=pltpu.CompilerParams(vmem_limit_bytes=VMEM_LIMIT),
        name="mix_out",
    )(x, mod, a, u, u,
      jnp.broadcast_to(conv_w[:, None, :], (CONV_KERNEL, SUBLANES, cw)),
      jnp.broadcast_to(conv_b[None, :], (SUBLANES, cw)), ln_g.reshape(1, cw),
      ln_b.reshape(1, cw), w_out_mix, g_post.reshape(1, d))


def kernel(x, c, w_ada, b_ada, g_pre_ff1, g_post_ff1, ff1_w_in, ff1_w_out, g_pre_mix, g_post_mix, w_in_mix, g_attn_out, conv_w, conv_b, conv_ln_g, conv_ln_b, w_out_mix, g_pre_ff2, g_post_ff2, ff2_w_in, ff2_w_out):
    mod = _adaln(c, w_ada, b_ada)
    h, (w_in_mix_b, w_out_mix_b, ff2_w_in_b, ff2_w_out_b) = _ffn(
        x, mod, g_pre_ff1, g_post_ff1, ff1_w_in.astype(BF16), ff1_w_out.astype(BF16),
        0, 0.5, cast_along=(w_in_mix, w_out_mix, ff2_w_in, ff2_w_out))
    qkv, u = _mix_in(h, mod, g_pre_mix, w_in_mix_b)
    a = _attention(qkv, g_attn_out)
    h = _mix_out(h, mod, a, u, conv_w, conv_b, conv_ln_g, conv_ln_b, w_out_mix_b,
                 g_post_mix)
    h, _ = _ffn(h, mod, g_pre_ff2, g_post_ff2, ff2_w_in_b, ff2_w_out_b, 2, 0.5)
    return h
```

```python
import functools

import jax
import jax.numpy as jnp
from jax import lax
from jax.experimental import pallas as pl
from jax.experimental.pallas import tpu as pltpu

D_MODEL = 1024
ATTN_WIDTH = D_MODEL // 2
HEAD_DIM = 64
N_HEADS = ATTN_WIDTH // HEAD_DIM
CONV_WIDTH_CH = D_MODEL - ATTN_WIDTH
CONV_KERNEL = 31
D_FF = 2816
N_SUBLAYERS = 3
MIX_IN = 3 * ATTN_WIDTH + 2 * CONV_WIDTH_CH
RMS_EPS = 1e-6
LN_EPS = 1e-5

LANES = 128
SUBLANES = 8
MXU_DIM = 256
VMEM_LIMIT = 56 * 1024 * 1024

FFN_TM = 1024
FFN_SUB = 256
FFN_CHUNK = D_FF
MIX_TM = 1024
MIX_SUB = 256
MIXOUT_TM = 256
MIXOUT_SUB = 256
CONV_ROWS = 32
HALO = 32
SHIFT_SPAN = HALO - SUBLANES
ATT_Q = 64
ATT_BLOCKS = 8
ATT_PAIRS = ATTN_WIDTH // LANES
ATT_K0 = 256
ATT_BACK = ATT_K0 - ATT_Q
ATT_KB = ATT_Q
LOGW_FLOOR = -88.0
LOG2E = 1.4426950408889634

F32 = jnp.float32
BF16 = jnp.bfloat16


def _sigmoid(x):
    return 1.0 / (1.0 + jnp.exp(-x))


def _dot(a, b):
    return jnp.dot(a, b, preferred_element_type=F32)


def _rms(x, g):
    ms = jnp.mean(x * x, axis=-1, keepdims=True)
    return x * lax.rsqrt(ms + RMS_EPS) * g


def _adaln_kernel(c_ref, w_ref, b_ref, o_ref):
    c = c_ref[...]
    s = c * _sigmoid(c)
    o_ref[...] = _dot(s.astype(BF16), w_ref[...].astype(BF16)) + b_ref[...]


def _adaln(c, w_ada, b_ada):
    bsz = c.shape[0]
    n = w_ada.shape[1]
    tn = 1024
    c_pad = jnp.zeros((SUBLANES, D_MODEL), F32).at[:bsz].set(c)
    out = pl.pallas_call(
        _adaln_kernel,
        grid=(n // tn,),
        in_specs=[
            pl.BlockSpec((SUBLANES, D_MODEL), lambda j: (0, 0)),
            pl.BlockSpec((D_MODEL, tn), lambda j: (0, j)),
            pl.BlockSpec((1, tn), lambda j: (0, j)),
        ],
        out_specs=pl.BlockSpec((SUBLANES, tn), lambda j: (0, j)),
        out_shape=jax.ShapeDtypeStruct((SUBLANES, n), F32),
        name="adaln",
    )(c_pad, w_ada, b_ada.reshape(1, n))
    return out[:bsz].reshape(bsz, 3 * N_SUBLAYERS, D_MODEL)


def _modulated(x, mod_ref, sub, g_pre):
    shift = mod_ref[0, 3 * sub:3 * sub + 1, :]
    scale = mod_ref[0, 3 * sub + 1:3 * sub + 2, :]
    return _rms(x, g_pre) * (1.0 + scale) + shift


def _ffn_rows(x, mod_ref, gpre_ref, gpost_ref, win_ref, wout_ref, sub, res_w):
    hb = _modulated(x, mod_ref, sub, gpre_ref[...]).astype(BF16)
    f = None
    for c in range(D_FF // FFN_CHUNK):
        lo = c * FFN_CHUNK
        g = _dot(hb, win_ref[:, lo:lo + FFN_CHUNK])
        u = _dot(hb, win_ref[:, D_FF + lo:D_FF + lo + FFN_CHUNK])
        act = (g * _sigmoid(g) * u).astype(BF16)
        part = _dot(act, wout_ref[lo:lo + FFN_CHUNK, :])
        f = part if f is None else f + part
    y = _rms(f, gpost_ref[...])
    gate = mod_ref[0, 3 * sub + 2:3 * sub + 3, :]
    return x + res_w * (1.0 + gate) * y


def _ffn_kernel(x_ref, mod_ref, gpre_ref, gpost_ref, win_ref, wout_ref, *rest,
                sub, res_w):
    n_cast = len(rest) // 2
    o_ref = rest[n_cast]
    for s in range(FFN_TM // FFN_SUB):
        rows = slice(s * FFN_SUB, (s + 1) * FFN_SUB)
        o_ref[0, rows, :] = _ffn_rows(x_ref[0, rows, :], mod_ref, gpre_ref, gpost_ref,
                                      win_ref, wout_ref, sub, res_w)
    for src_ref, dst_ref in zip(rest[:n_cast], rest[n_cast + 1:]):
        dst_ref[...] = src_ref[...].astype(BF16)


def _const_spec(shape):
    nd = len(shape)
    return pl.BlockSpec(shape, lambda *_: (0,) * nd, pipeline_mode=pl.Buffered(1))


def _row_blocks(rows, steps):
    bf16_rows = 2 * SUBLANES
    return max(n for n in range(1, steps + 1)
               if rows % n == 0 and (rows // n) % bf16_rows == 0)


def _ffn(x, mod, g_pre, g_post, w_in, w_out, sub, res_w, cast_along=()):
    bsz, seq, d = x.shape
    tm = FFN_TM
    n_i = seq // tm
    cast_specs, cast_shapes = [], []
    for w in cast_along:
        n_blk = _row_blocks(w.shape[0], bsz * n_i)
        spec = pl.BlockSpec(
            (w.shape[0] // n_blk, w.shape[1]),
            lambda b, i, n_blk=n_blk: (jnp.minimum(b * n_i + i, n_blk - 1), 0))
        cast_specs.append(spec)
        cast_shapes.append(jax.ShapeDtypeStruct(w.shape, BF16))
    out = pl.pallas_call(
        functools.partial(_ffn_kernel, sub=sub, res_w=res_w),
        grid=(bsz, n_i),
        in_specs=[
            pl.BlockSpec((1, tm, d), lambda b, i: (b, i, 0)),
            pl.BlockSpec((1, 3 * N_SUBLAYERS, d), lambda b, i: (b, 0, 0)),
            _const_spec((1, d)),
            _const_spec((1, d)),
            _const_spec((d, 2 * D_FF)),
            _const_spec((D_FF, d)),
        ] + cast_specs,
        out_specs=[pl.BlockSpec((1, tm, d), lambda b, i: (b, i, 0))] + cast_specs,
        out_shape=[jax.ShapeDtypeStruct(x.shape, F32)] + cast_shapes,
        compiler_params=pltpu.CompilerParams(
            dimension_semantics=("arbitrary", "arbitrary"),
            vmem_limit_bytes=VMEM_LIMIT),
        name=f"ffn{sub}",
    )(x, mod, g_pre.reshape(1, d), g_post.reshape(1, d), w_in, w_out, *cast_along)
    return out[0], out[1:]


def _mixin_kernel(x_ref, mod_ref, gpre_ref, w_ref, qkv_ref, u_ref):
    aw = ATTN_WIDTH
    for s in range(MIX_TM // MIX_SUB):
        rows = slice(s * MIX_SUB, (s + 1) * MIX_SUB)
        hb = _modulated(x_ref[0, rows, :], mod_ref, 1, gpre_ref[...]).astype(BF16)
        q = _dot(hb, w_ref[:, 0:aw]) * (HEAD_DIM ** -0.5)
        qkv_ref[0, rows, 0:aw] = q.astype(BF16)
        kv = _dot(hb, w_ref[:, aw:3 * aw])
        qkv_ref[0, rows, aw:3 * aw] = kv.astype(BF16)
        cv = _dot(hb, w_ref[:, 3 * aw:3 * aw + CONV_WIDTH_CH])
        cg = _dot(hb, w_ref[:, 3 * aw + CONV_WIDTH_CH:MIX_IN])
        u_ref[0, rows, :] = cv * _sigmoid(cg)


def _mix_in(x, mod, g_pre, w_in_mix):
    bsz, seq, d = x.shape
    tm = MIX_TM
    return pl.pallas_call(
        _mixin_kernel,
        grid=(bsz, seq // tm),
        in_specs=[
            pl.BlockSpec((1, tm, d), lambda b, i: (b, i, 0)),
            pl.BlockSpec((1, 3 * N_SUBLAYERS, d), lambda b, i: (b, 0, 0)),
            _const_spec((1, d)),
            _const_spec((d, MIX_IN)),
        ],
        out_specs=[
            pl.BlockSpec((1, tm, 3 * ATTN_WIDTH), lambda b, i: (b, i, 0)),
            pl.BlockSpec((1, tm, CONV_WIDTH_CH), lambda b, i: (b, i, 0)),
        ],
        out_shape=[
            jax.ShapeDtypeStruct((bsz, seq, 3 * ATTN_WIDTH), BF16),
            jax.ShapeDtypeStruct((bsz, seq, CONV_WIDTH_CH), F32),
        ],
        compiler_params=pltpu.CompilerParams(vmem_limit_bytes=VMEM_LIMIT),
        name="mix_in",
    )(x, mod, g_pre.reshape(1, d), w_in_mix)


def _softplus(z):
    t = jnp.exp2(jnp.abs(z) * (-LOG2E))
    return jnp.maximum(z, 0.0) + jnp.log(1.0 + t)


def _qk(qm, k):
    return lax.dot_general(qm, k, (((1,), (1,)), ((), ())),
                           preferred_element_type=F32)


def _attn_kernel(q_ref, k_ref, v_ref, g_ref, o_ref, tri_ref, z_ref, sp_ref, after_ref):
    b = pl.program_id(0)
    step = pl.program_id(1)
    back_chunks = ATT_BACK // ATT_Q

    @pl.when((b == 0) & (step == 0))
    def _():
        r = lax.broadcasted_iota(jnp.int32, (ATT_K0, ATT_K0), 0)
        c = lax.broadcasted_iota(jnp.int32, (ATT_K0, ATT_K0), 1)
        for missing in range(back_chunks + 1):
            dead = (r >= ATT_BACK - missing * ATT_Q) & (r < ATT_BACK)
            tri_ref[missing] = jnp.where((r > c) & jnp.logical_not(dead), -1.0, 0.0).astype(BF16)

    first = lax.broadcasted_iota(jnp.int32, (1, LANES), 1) < HEAD_DIM
    rows2 = 2 * ATT_Q
    row = lax.broadcasted_iota(jnp.int32, (rows2, LANES), 0)
    col = lax.broadcasted_iota(jnp.int32, (rows2, LANES), 1) + (ATT_K0 - LANES)
    causal = (col - ATT_BACK) < jnp.bitwise_and(row, ATT_Q - 1)
    tri_b = tri_ref[0, 0:ATT_KB, 0:ATT_KB]

    def stacked_q(j, p):
        qp = q_ref[0, j * ATT_Q:(j + 1) * ATT_Q, p * LANES:(p + 1) * LANES]
        zero = jnp.zeros_like(qp)
        return jnp.concatenate(
            [jnp.where(first, qp, zero), jnp.where(first, zero, qp)], axis=0)

    def weighted_values(w, v):
        wb = w.astype(BF16)
        return jnp.where(first, _dot(wb[0:ATT_Q], v), _dot(wb[ATT_Q:rows2], v))

    def unfinished(carries):
        worst = functools.reduce(jnp.maximum, carries)
        return (jnp.max(worst) > LOGW_FLOOR).astype(jnp.int32)

    def write_normalized(j, accs):
        for p in range(ATT_PAIRS):
            lanes = slice(p * LANES, (p + 1) * LANES)
            o = accs[p]
            sq = o * o
            s_all = jnp.sum(sq, axis=-1, keepdims=True)
            s_first = jnp.sum(jnp.where(first, sq, 0.0), axis=-1, keepdims=True)
            ms = jnp.where(first, s_first, s_all - s_first) * (1.0 / HEAD_DIM)
            o_ref[0, j * ATT_Q:(j + 1) * ATT_Q, lanes] = (
                o * lax.rsqrt(ms + RMS_EPS) * g_ref[:, lanes]).astype(BF16)

    blks = [step * ATT_BLOCKS + j for j in range(ATT_BLOCKS)]
    back_rows = [pl.multiple_of(jnp.maximum(blk * ATT_Q - ATT_BACK, 0), ATT_Q) for blk in blks]
    diag_rows = [pl.multiple_of(blk * ATT_Q, ATT_Q) for blk in blks]

    def window_keys(j, lanes):
        return jnp.concatenate([k_ref[0, pl.ds(back_rows[j], ATT_BACK), lanes],
                                k_ref[0, pl.ds(diag_rows[j], ATT_Q), lanes]], axis=0)

    def window_values(j, lanes):
        chunks = []
        for i in range(back_chunks):
            rows = pl.multiple_of(back_rows[j] + i * ATT_Q, ATT_Q)
            chunk = v_ref[0, pl.ds(rows, ATT_Q), lanes]
            chunks.append(jnp.where(blks[j] > i, chunk, jnp.zeros_like(chunk)))
        chunks.append(v_ref[0, pl.ds(diag_rows[j], ATT_Q), lanes])
        return jnp.concatenate(chunks, axis=0)

    def window_softplus(z):
        sp = _softplus(z)
        return sp[:, 0:LANES], jnp.where(causal, sp[:, LANES:ATT_K0], 0.0), sp

    for j in range(ATT_BLOCKS):
        for p in range(ATT_PAIRS):
            lanes = slice(p * LANES, (p + 1) * LANES)
            z_ref[j * ATT_PAIRS + p] = _qk(stacked_q(j, p), window_keys(j, lanes))
    least = [[] for _ in range(ATT_BLOCKS)]
    for j in range(ATT_BLOCKS):
        tri = tri_ref[jnp.maximum(back_chunks - blks[j], 0)]
        for p in range(ATT_PAIRS):
            slot = j * ATT_PAIRS + p
            z = z_ref[slot]
            sp_lo, sp_hi, sp = window_softplus(z)
            z_ref[slot] = z - sp
            least[j].append(jnp.min(jnp.sum(sp_lo + sp_hi, axis=-1, keepdims=True)))
            sp_ref[slot, :, 0:LANES] = sp_lo.astype(BF16)
            sp_ref[slot, :, LANES:ATT_K0] = sp_hi.astype(BF16)
            after_ref[slot] = _dot(sp_ref[slot], tri)
    blocks = []
    for j in range(ATT_BLOCKS):
        accs = []
        for p in range(ATT_PAIRS):
            slot = j * ATT_PAIRS + p
            lanes = slice(p * LANES, (p + 1) * LANES)
            w = jnp.exp(z_ref[slot] + after_ref[slot])
            w = jnp.concatenate(
                [w[:, 0:LANES], jnp.where(causal, w[:, LANES:ATT_K0], 0.0)], axis=1)
            accs.append(weighted_values(w, window_values(j, lanes)))
        write_normalized(j, accs)
        blocks.append((blks[j], functools.reduce(jnp.minimum, least[j]) < -LOGW_FLOOR, accs))

    for j, (blk, more, accs) in enumerate(blocks):
        @pl.when(jnp.logical_and(blk > back_chunks, more))
        def _(j=j, blk=blk, accs=accs):
            carries = []
            for p in range(ATT_PAIRS):
                lanes = slice(p * LANES, (p + 1) * LANES)
                z = _qk(stacked_q(j, p), window_keys(j, lanes))
                sp_lo, sp_hi, _ = window_softplus(z)
                carries.append(-jnp.sum(sp_lo + sp_hi, axis=-1, keepdims=True))

            def cond(st):
                jb, more, _, _ = st
                return jnp.logical_and(jb >= 0, more > 0)

            def body(st):
                jb, _, carries, accs = st
                ks = pl.multiple_of(jb * ATT_KB, ATT_KB)
                new_c, new_a = [], []
                for p in range(ATT_PAIRS):
                    lanes = slice(p * LANES, (p + 1) * LANES)
                    kb = k_ref[0, pl.ds(ks, ATT_KB), lanes]
                    vb = v_ref[0, pl.ds(ks, ATT_KB), lanes]
                    z = _qk(stacked_q(j, p), kb)
                    sp = _softplus(z)
                    w = jnp.exp((z - sp) + _dot(sp.astype(BF16), tri_b) + carries[p])
                    new_a.append(accs[p] + weighted_values(w, vb))
                    new_c.append(carries[p] - jnp.sum(sp, axis=-1, keepdims=True))
                return jb - 1, unfinished(new_c), tuple(new_c), tuple(new_a)

            init = (blk - back_chunks - 1, jnp.int32(1), tuple(carries), tuple(accs))
            write_normalized(j, lax.while_loop(cond, body, init)[3])


def _attention(qkv, g_attn_out):
    bsz, seq, _ = qkv.shape
    tq = ATT_BLOCKS * ATT_Q
    slots = ATT_BLOCKS * ATT_PAIRS
    kv_spec = functools.partial(pl.BlockSpec, (1, seq, ATTN_WIDTH))
    return pl.pallas_call(
        _attn_kernel,
        grid=(bsz, seq // tq),
        in_specs=[
            pl.BlockSpec((1, tq, ATTN_WIDTH), lambda b, i: (b, i, 0)),
            kv_spec(lambda b, i: (b, 0, 1)),
            kv_spec(lambda b, i: (b, 0, 2)),
            _const_spec((1, ATTN_WIDTH)),
        ],
        out_specs=pl.BlockSpec((1, tq, ATTN_WIDTH), lambda b, i: (b, i, 0)),
        out_shape=jax.ShapeDtypeStruct((bsz, seq, ATTN_WIDTH), BF16),
        scratch_shapes=[
            pltpu.VMEM((ATT_BACK // ATT_Q + 1, ATT_K0, ATT_K0), BF16),
            pltpu.VMEM((slots, 2 * ATT_Q, ATT_K0), F32),
            pltpu.VMEM((slots, 2 * ATT_Q, ATT_K0), BF16),
            pltpu.VMEM((slots, 2 * ATT_Q, ATT_K0), F32),
        ],
        compiler_params=pltpu.CompilerParams(
            dimension_semantics=("arbitrary", "arbitrary"),
            vmem_limit_bytes=VMEM_LIMIT),
        name="attn",
    )(qkv, qkv, qkv, g_attn_out.reshape(1, ATTN_WIDTH))


def _conv_branch_tasks(buf_ref, shift_ref, cw_ref, cb_ref, lng_ref, lnb_ref, act_ref):
    lead = HALO - (CONV_KERNEL - 1)
    groups = CONV_ROWS // SUBLANES

    def fill_shifts(r0):
        for s in range(1, SUBLANES):
            shift_ref[s - 1] = buf_ref[r0 + s:r0 + s + MIXOUT_SUB + SHIFT_SPAN, :]

    def chunk(r0, r):
        y = jnp.broadcast_to(cb_ref[...][None], (groups, SUBLANES, CONV_WIDTH_CH))
        for t in range(CONV_KERNEL):
            s = (lead + t) % SUBLANES
            base = r * CONV_ROWS + lead + t - s
            if s == 0:
                rows = buf_ref[r0 + base:r0 + base + CONV_ROWS, :]
            else:
                rows = shift_ref[s - 1, base:base + CONV_ROWS, :]
            y = y + cw_ref[t][None] * rows.reshape(groups, SUBLANES, CONV_WIDTH_CH)
        y = y.reshape(CONV_ROWS, CONV_WIDTH_CH)
        mu = jnp.mean(y, axis=-1, keepdims=True)
        yc = y - mu
        var = jnp.mean(yc * yc, axis=-1, keepdims=True)
        yn = yc * lax.rsqrt(var + LN_EPS) * lng_ref[...] + lnb_ref[...]
        act_ref[r0 + r * CONV_ROWS:r0 + (r + 1) * CONV_ROWS, :] = (
            yn * _sigmoid(yn)).astype(BF16)

    tasks = []
    for r0 in range(0, MIXOUT_TM, MIXOUT_SUB):
        tasks.append(functools.partial(fill_shifts, r0))
        for r in range(MIXOUT_SUB // CONV_ROWS):
            tasks.append(functools.partial(chunk, r0, r))
    return tasks


def _mixout_kernel(x_ref, mod_ref, a_ref, u_ref, halo_ref, cw_ref, cb_ref, lng_ref,
                   lnb_ref, w_ref, gpost_ref, o_ref, buf_ref, shift_ref, act_ref):
    i = pl.program_id(1)
    has_prev = (i > 0).astype(F32)
    buf_ref[0:HALO, :] = halo_ref[0] * has_prev
    buf_ref[HALO:HALO + MIXOUT_TM, :] = u_ref[0]
    for task in _conv_branch_tasks(buf_ref, shift_ref, cw_ref, cb_ref, lng_ref, lnb_ref,
                                   act_ref):
        task()
    gate = mod_ref[0, 5:6, :]
    for s in range(MIXOUT_TM // MIXOUT_SUB):
        rows = slice(s * MIXOUT_SUB, (s + 1) * MIXOUT_SUB)
        m = (_dot(a_ref[0, rows, :], w_ref[0:ATTN_WIDTH, :])
             + _dot(act_ref[rows, :], w_ref[ATTN_WIDTH:D_MODEL, :]))
        o_ref[0, rows, :] = x_ref[0, rows, :] + (1.0 + gate) * _rms(m, gpost_ref[...])


def _mix_out(x, mod, a, u, conv_w, conv_b, ln_g, ln_b, w_out_mix, g_post):
    bsz, seq, d = x.shape
    tm = MIXOUT_TM
    cw = CONV_WIDTH_CH
    per = tm // HALO
    return pl.pallas_call(
        _mixout_kernel,
        grid=(bsz, seq // tm),
        in_specs=[
            pl.BlockSpec((1, tm, d), lambda b, i: (b, i, 0)),
            pl.BlockSpec((1, 3 * N_SUBLAYERS, d), lambda b, i: (b, 0, 0)),
            pl.BlockSpec((1, tm, ATTN_WIDTH), lambda b, i: (b, i, 0)),
            pl.BlockSpec((1, tm, cw), lambda b, i: (b, i, 0)),
            pl.BlockSpec((1, HALO, cw), lambda b, i: (b, jnp.maximum(i * per - 1, 0), 0)),
            _const_spec((CONV_KERNEL, SUBLANES, cw)),
            _const_spec((SUBLANES, cw)),
            _const_spec((1, cw)),
            _const_spec((1, cw)),
            _const_spec((d, d)),
            _const_spec((1, d)),
        ],
        out_specs=pl.BlockSpec((1, tm, d), lambda b, i: (b, i, 0)),
        out_shape=jax.ShapeDtypeStruct(x.shape, F32),
        scratch_shapes=[
            pltpu.VMEM((HALO + tm, cw), F32),
            pltpu.VMEM((SUBLANES - 1, MIXOUT_SUB + SHIFT_SPAN, cw), F32),
            pltpu.VMEM((tm, cw), BF16),
        ],
        compiler_params=pltpu.CompilerParams(vmem_limit_bytes=VMEM_LIMIT),
        name="mix_out",
    )(x, mod, a, u, u,
      jnp.broadcast_to(conv_w[:, None, :], (CONV_KERNEL, SUBLANES, cw)),
      jnp.broadcast_to(conv_b[None, :], (SUBLANES, cw)), ln_g.reshape(1, cw),
      ln_b.reshape(1, cw), w_out_mix, g_post.reshape(1, d))


def kernel(x, c, w_ada, b_ada, g_pre_ff1, g_post_ff1, ff1_w_in, ff1_w_out, g_pre_mix, g_post_mix, w_in_mix, g_attn_out, conv_w, conv_b, conv_ln_g, conv_ln_b, w_out_mix, g_pre_ff2, g_post_ff2, ff2_w_in, ff2_w_out):
    mod = _adaln(c, w_ada, b_ada)
    h, (w_in_mix_b, w_out_mix_b, ff2_w_in_b, ff2_w_out_b) = _ffn(
        x, mod, g_pre_ff1, g_post_ff1, ff1_w_in.astype(BF16), ff1_w_out.astype(BF16),
        0, 0.5, cast_along=(w_in_mix, w_out_mix, ff2_w_in, ff2_w_out))
    qkv, u = _mix_in(h, mod, g_pre_mix, w_in_mix_b)
    a = _attention(qkv, g_attn_out)
    h = _mix_out(h, mod, a, u, conv_w, conv_b, conv_ln_g, conv_ln_b, w_out_mix_b,
                 g_post_mix)
    h, _ = _ffn(h, mod, g_pre_ff2, g_post_ff2, ff2_w_in_b, ff2_w_out_b, 2, 0.5)
    return h
```

```python
import functools

import jax
import jax.numpy as jnp
from jax import lax
from jax.experimental import pallas as pl
from jax.experimental.pallas import tpu as pltpu

D_MODEL = 1024
ATTN_WIDTH = D_MODEL // 2
HEAD_DIM = 64
N_HEADS = ATTN_WIDTH // HEAD_DIM
CONV_WIDTH_CH = D_MODEL - ATTN_WIDTH
CONV_KERNEL = 31
D_FF = 2816
N_SUBLAYERS = 3
MIX_IN = 3 * ATTN_WIDTH + 2 * CONV_WIDTH_CH
RMS_EPS = 1e-6
LN_EPS = 1e-5

LANES = 128
SUBLANES = 8
MXU_DIM = 256
VMEM_LIMIT = 56 * 1024 * 1024

FFN_TM = 1024
FFN_SUB = 256
FFN_CHUNK = D_FF
MIX_TM = 1024
MIX_SUB = 256
MIXOUT_TM = 512
MIXOUT_SUB = 256
CONV_ROWS = 32
HALO = 32
SHIFT_SPAN = HALO - SUBLANES
ATT_Q = 64
ATT_BLOCKS = 8
ATT_PAIRS = ATTN_WIDTH // LANES
ATT_K0 = 256
ATT_BACK = ATT_K0 - ATT_Q
ATT_KB = ATT_Q
LOGW_FLOOR = -88.0
LOG2E = 1.4426950408889634

F32 = jnp.float32
BF16 = jnp.bfloat16


def _sigmoid(x):
    return 1.0 / (1.0 + jnp.exp(-x))


def _dot(a, b):
    return jnp.dot(a, b, preferred_element_type=F32)


def _rms(x, g):
    ms = jnp.mean(x * x, axis=-1, keepdims=True)
    return x * lax.rsqrt(ms + RMS_EPS) * g


def _adaln_kernel(c_ref, w_ref, b_ref, o_ref):
    c = c_ref[...]
    s = c * _sigmoid(c)
    o_ref[...] = _dot(s.astype(BF16), w_ref[...].astype(BF16)) + b_ref[...]


def _adaln(c, w_ada, b_ada):
    bsz = c.shape[0]
    n = w_ada.shape[1]
    tn = 1024
    c_pad = jnp.zeros((SUBLANES, D_MODEL), F32).at[:bsz].set(c)
    out = pl.pallas_call(
        _adaln_kernel,
        grid=(n // tn,),
        in_specs=[
            pl.BlockSpec((SUBLANES, D_MODEL), lambda j: (0, 0)),
            pl.BlockSpec((D_MODEL, tn), lambda j: (0, j)),
            pl.BlockSpec((1, tn), lambda j: (0, j)),
        ],
        out_specs=pl.BlockSpec((SUBLANES, tn), lambda j: (0, j)),
        out_shape=jax.ShapeDtypeStruct((SUBLANES, n), F32),
        name="adaln",
    )(c_pad, w_ada, b_ada.reshape(1, n))
    return out[:bsz].reshape(bsz, 3 * N_SUBLAYERS, D_MODEL)


def _modulated(x, mod_ref, sub, g_pre):
    shift = mod_ref[0, 3 * sub:3 * sub + 1, :]
    scale = mod_ref[0, 3 * sub + 1:3 * sub + 2, :]
    return _rms(x, g_pre) * (1.0 + scale) + shift


def _ffn_rows(x, mod_ref, gpre_ref, gpost_ref, win_ref, wout_ref, sub, res_w):
    hb = _modulated(x, mod_ref, sub, gpre_ref[...]).astype(BF16)
    f = None
    for c in range(D_FF // FFN_CHUNK):
        lo = c * FFN_CHUNK
        g = _dot(hb, win_ref[:, lo:lo + FFN_CHUNK])
        u = _dot(hb, win_ref[:, D_FF + lo:D_FF + lo + FFN_CHUNK])
        act = (g * _sigmoid(g) * u).astype(BF16)
        part = _dot(act, wout_ref[lo:lo + FFN_CHUNK, :])
        f = part if f is None else f + part
    y = _rms(f, gpost_ref[...])
    gate = mod_ref[0, 3 * sub + 2:3 * sub + 3, :]
    return x + res_w * (1.0 + gate) * y


def _ffn_kernel(x_ref, mod_ref, gpre_ref, gpost_ref, win_ref, wout_ref, *rest,
                sub, res_w):
    n_cast = len(rest) // 2
    o_ref = rest[n_cast]
    for s in range(FFN_TM // FFN_SUB):
        rows = slice(s * FFN_SUB, (s + 1) * FFN_SUB)
        o_ref[0, rows, :] = _ffn_rows(x_ref[0, rows, :], mod_ref, gpre_ref, gpost_ref,
                                      win_ref, wout_ref, sub, res_w)
    for src_ref, dst_ref in zip(rest[:n_cast], rest[n_cast + 1:]):
        dst_ref[...] = src_ref[...].astype(BF16)


def _const_spec(shape):
    nd = len(shape)
    return pl.BlockSpec(shape, lambda *_: (0,) * nd, pipeline_mode=pl.Buffered(1))


def _row_blocks(rows, steps):
    bf16_rows = 2 * SUBLANES
    return max(n for n in range(1, steps + 1)
               if rows % n == 0 and (rows // n) % bf16_rows == 0)


def _ffn(x, mod, g_pre, g_post, w_in, w_out, sub, res_w, cast_along=()):
    bsz, seq, d = x.shape
    tm = FFN_TM
    n_i = seq // tm
    cast_specs, cast_shapes = [], []
    for w in cast_along:
        n_blk = _row_blocks(w.shape[0], bsz * n_i)
        spec = pl.BlockSpec(
            (w.shape[0] // n_blk, w.shape[1]),
            lambda b, i, n_blk=n_blk: (jnp.minimum(b * n_i + i, n_blk - 1), 0))
        cast_specs.append(spec)
        cast_shapes.append(jax.ShapeDtypeStruct(w.shape, BF16))
    out = pl.pallas_call(
        functools.partial(_ffn_kernel, sub=sub, res_w=res_w),
        grid=(bsz, n_i),
        in_specs=[
            pl.BlockSpec((1, tm, d), lambda b, i: (b, i, 0)),
            pl.BlockSpec((1, 3 * N_SUBLAYERS, d), lambda b, i: (b, 0, 0)),
            _const_spec((1, d)),
            _const_spec((1, d)),
            _const_spec((d, 2 * D_FF)),
            _const_spec((D_FF, d)),
        ] + cast_specs,
        out_specs=[pl.BlockSpec((1, tm, d), lambda b, i: (b, i, 0))] + cast_specs,
        out_shape=[jax.ShapeDtypeStruct(x.shape, F32)] + cast_shapes,
        compiler_params=pltpu.CompilerParams(
            dimension_semantics=("arbitrary", "arbitrary"),
            vmem_limit_bytes=VMEM_LIMIT),
        name=f"ffn{sub}",
    )(x, mod, g_pre.reshape(1, d), g_post.reshape(1, d), w_in, w_out, *cast_along)
    return out[0], out[1:]


def _mixin_kernel(x_ref, mod_ref, gpre_ref, w_ref, qkv_ref, u_ref):
    aw = ATTN_WIDTH
    for s in range(MIX_TM // MIX_SUB):
        rows = slice(s * MIX_SUB, (s + 1) * MIX_SUB)
        hb = _modulated(x_ref[0, rows, :], mod_ref, 1, gpre_ref[...]).astype(BF16)
        q = _dot(hb, w_ref[:, 0:aw]) * (HEAD_DIM ** -0.5)
        qkv_ref[0, rows, 0:aw] = q.astype(BF16)
        kv = _dot(hb, w_ref[:, aw:3 * aw])
        qkv_ref[0, rows, aw:3 * aw] = kv.astype(BF16)
        cv = _dot(hb, w_ref[:, 3 * aw:3 * aw + CONV_WIDTH_CH])
        cg = _dot(hb, w_ref[:, 3 * aw + CONV_WIDTH_CH:MIX_IN])
        u_ref[0, rows, :] = cv * _sigmoid(cg)


def _mix_in(x, mod, g_pre, w_in_mix):
    bsz, seq, d = x.shape
    tm = MIX_TM
    return pl.pallas_call(
        _mixin_kernel,
        grid=(bsz, seq // tm),
        in_specs=[
            pl.BlockSpec((1, tm, d), lambda b, i: (b, i, 0)),
            pl.BlockSpec((1, 3 * N_SUBLAYERS, d), lambda b, i: (b, 0, 0)),
            _const_spec((1, d)),
            _const_spec((d, MIX_IN)),
        ],
        out_specs=[
            pl.BlockSpec((1, tm, 3 * ATTN_WIDTH), lambda b, i: (b, i, 0)),
            pl.BlockSpec((1, tm, CONV_WIDTH_CH), lambda b, i: (b, i, 0)),
        ],
        out_shape=[
            jax.ShapeDtypeStruct((bsz, seq, 3 * ATTN_WIDTH), BF16),
            jax.ShapeDtypeStruct((bsz, seq, CONV_WIDTH_CH), F32),
        ],
        compiler_params=pltpu.CompilerParams(vmem_limit_bytes=VMEM_LIMIT),
        name="mix_in",
    )(x, mod, g_pre.reshape(1, d), w_in_mix)


def _softplus(z):
    t = jnp.exp2(jnp.abs(z) * (-LOG2E))
    return jnp.maximum(z, 0.0) + jnp.log(1.0 + t)


def _qk(qm, k):
    return lax.dot_general(qm, k, (((1,), (1,)), ((), ())),
                           preferred_element_type=F32)


def _attn_kernel(q_ref, k_ref, v_ref, g_ref, o_ref, tri_ref, z_ref, sp_ref, after_ref):
    b = pl.program_id(0)
    step = pl.program_id(1)
    back_chunks = ATT_BACK // ATT_Q

    @pl.when((b == 0) & (step == 0))
    def _():
        r = lax.broadcasted_iota(jnp.int32, (ATT_K0, ATT_K0), 0)
        c = lax.broadcasted_iota(jnp.int32, (ATT_K0, ATT_K0), 1)
        for missing in range(back_chunks + 1):
            dead = (r >= ATT_BACK - missing * ATT_Q) & (r < ATT_BACK)
            tri_ref[missing] = jnp.where((r > c) & jnp.logical_not(dead), -1.0, 0.0).astype(BF16)

    first = lax.broadcasted_iota(jnp.int32, (1, LANES), 1) < HEAD_DIM
    rows2 = 2 * ATT_Q
    row = lax.broadcasted_iota(jnp.int32, (rows2, LANES), 0)
    col = lax.broadcasted_iota(jnp.int32, (rows2, LANES), 1) + (ATT_K0 - LANES)
    causal = (col - ATT_BACK) < jnp.bitwise_and(row, ATT_Q - 1)
    tri_b = tri_ref[0, 0:ATT_KB, 0:ATT_KB]

    def stacked_q(j, p):
        qp = q_ref[0, j * ATT_Q:(j + 1) * ATT_Q, p * LANES:(p + 1) * LANES]
        zero = jnp.zeros_like(qp)
        return jnp.concatenate(
            [jnp.where(first, qp, zero), jnp.where(first, zero, qp)], axis=0)

    def weighted_values(w, v):
        wb = w.astype(BF16)
        return jnp.where(first, _dot(wb[0:ATT_Q], v), _dot(wb[ATT_Q:rows2], v))

    def unfinished(carries):
        worst = functools.reduce(jnp.maximum, carries)
        return (jnp.max(worst) > LOGW_FLOOR).astype(jnp.int32)

    def write_normalized(j, accs):
        for p in range(ATT_PAIRS):
            lanes = slice(p * LANES, (p + 1) * LANES)
            o = accs[p]
            sq = o * o
            s_all = jnp.sum(sq, axis=-1, keepdims=True)
            s_first = jnp.sum(jnp.where(first, sq, 0.0), axis=-1, keepdims=True)
            ms = jnp.where(first, s_first, s_all - s_first) * (1.0 / HEAD_DIM)
            o_ref[0, j * ATT_Q:(j + 1) * ATT_Q, lanes] = (
                o * lax.rsqrt(ms + RMS_EPS) * g_ref[:, lanes]).astype(BF16)

    blks = [step * ATT_BLOCKS + j for j in range(ATT_BLOCKS)]
    back_rows = [pl.multiple_of(jnp.maximum(blk * ATT_Q - ATT_BACK, 0), ATT_Q) for blk in blks]
    diag_rows = [pl.multiple_of(blk * ATT_Q, ATT_Q) for blk in blks]

    def window_keys(j, lanes):
        return jnp.concatenate([k_ref[0, pl.ds(back_rows[j], ATT_BACK), lanes],
                                k_ref[0, pl.ds(diag_rows[j], ATT_Q), lanes]], axis=0)

    def window_values(j, lanes):
        chunks = []
        for i in range(back_chunks):
            rows = pl.multiple_of(back_rows[j] + i * ATT_Q, ATT_Q)
            chunk = v_ref[0, pl.ds(rows, ATT_Q), lanes]
            chunks.append(jnp.where(blks[j] > i, chunk, jnp.zeros_like(chunk)))
        chunks.append(v_ref[0, pl.ds(diag_rows[j], ATT_Q), lanes])
        return jnp.concatenate(chunks, axis=0)

    def window_softplus(z):
        sp = _softplus(z)
        return sp[:, 0:LANES], jnp.where(causal, sp[:, LANES:ATT_K0], 0.0), sp

    for j in range(ATT_BLOCKS):
        for p in range(ATT_PAIRS):
            lanes = slice(p * LANES, (p + 1) * LANES)
            z_ref[j * ATT_PAIRS + p] = _qk(stacked_q(j, p), window_keys(j, lanes))
    least = [[] for _ in range(ATT_BLOCKS)]
    for j in range(ATT_BLOCKS):
        tri = tri_ref[jnp.maximum(back_chunks - blks[j], 0)]
        for p in range(ATT_PAIRS):
            slot = j * ATT_PAIRS + p
            z = z_ref[slot]
            sp_lo, sp_hi, sp = window_softplus(z)
            z_ref[slot] = z - sp
            least[j].append(jnp.min(jnp.sum(sp_lo + sp_hi, axis=-1, keepdims=True)))
            sp_ref[slot, :, 0:LANES] = sp_lo.astype(BF16)
            sp_ref[slot, :, LANES:ATT_K0] = sp_hi.astype(BF16)
            after_ref[slot] = _dot(sp_ref[slot], tri)
    blocks = []
    for j in range(ATT_BLOCKS):
        accs = []
        for p in range(ATT_PAIRS):
            slot = j * ATT_PAIRS + p
            lanes = slice(p * LANES, (p + 1) * LANES)
            w = jnp.exp(z_ref[slot] + after_ref[slot])
            w = jnp.concatenate(
                [w[:, 0:LANES], jnp.where(causal, w[:, LANES:ATT_K0], 0.0)], axis=1)
            accs.append(weighted_values(w, window_values(j, lanes)))
        write_normalized(j, accs)
        blocks.append((blks[j], functools.reduce(jnp.minimum, least[j]) < -LOGW_FLOOR, accs))

    for j, (blk, more, accs) in enumerate(blocks):
        @pl.when(jnp.logical_and(blk > back_chunks, more))
        def _(j=j, blk=blk, accs=accs):
            carries = []
            for p in range(ATT_PAIRS):
                lanes = slice(p * LANES, (p + 1) * LANES)
                z = _qk(stacked_q(j, p), window_keys(j, lanes))
                sp_lo, sp_hi, _ = window_softplus(z)
                carries.append(-jnp.sum(sp_lo + sp_hi, axis=-1, keepdims=True))

            def cond(st):
                jb, more, _, _ = st
                return jnp.logical_and(jb >= 0, more > 0)

            def body(st):
                jb, _, carries, accs = st
                ks = pl.multiple_of(jb * ATT_KB, ATT_KB)
                new_c, new_a = [], []
                for p in range(ATT_PAIRS):
                    lanes = slice(p * LANES, (p + 1) * LANES)
                    kb = k_ref[0, pl.ds(ks, ATT_KB), lanes]
                    vb = v_ref[0, pl.ds(ks, ATT_KB), lanes]
                    z = _qk(stacked_q(j, p), kb)
                    sp = _softplus(z)
                    w = jnp.exp((z - sp) + _dot(sp.astype(BF16), tri_b) + carries[p])
                    new_a.append(accs[p] + weighted_values(w, vb))
                    new_c.append(carries[p] - jnp.sum(sp, axis=-1, keepdims=True))
                return jb - 1, unfinished(new_c), tuple(new_c), tuple(new_a)

            init = (blk - back_chunks - 1, jnp.int32(1), tuple(carries), tuple(accs))
            write_normalized(j, lax.while_loop(cond, body, init)[3])


def _attention(qkv, g_attn_out):
    bsz, seq, _ = qkv.shape
    tq = ATT_BLOCKS * ATT_Q
    slots = ATT_BLOCKS * ATT_PAIRS
    kv_spec = functools.partial(pl.BlockSpec, (1, seq, ATTN_WIDTH))
    return pl.pallas_call(
        _attn_kernel,
        grid=(bsz, seq // tq),
        in_specs=[
            pl.BlockSpec((1, tq, ATTN_WIDTH), lambda b, i: (b, i, 0)),
            kv_spec(lambda b, i: (b, 0, 1)),
            kv_spec(lambda b, i: (b, 0, 2)),
            _const_spec((1, ATTN_WIDTH)),
        ],
        out_specs=pl.BlockSpec((1, tq, ATTN_WIDTH), lambda b, i: (b, i, 0)),
        out_shape=jax.ShapeDtypeStruct((bsz, seq, ATTN_WIDTH), BF16),
        scratch_shapes=[
            pltpu.VMEM((ATT_BACK // ATT_Q + 1, ATT_K0, ATT_K0), BF16),
            pltpu.VMEM((slots, 2 * ATT_Q, ATT_K0), F32),
            pltpu.VMEM((slots, 2 * ATT_Q, ATT_K0), BF16),
            pltpu.VMEM((slots, 2 * ATT_Q, ATT_K0), F32),
        ],
        compiler_params=pltpu.CompilerParams(
            dimension_semantics=("arbitrary", "arbitrary"),
            vmem_limit_bytes=VMEM_LIMIT),
        name="attn",
    )(qkv, qkv, qkv, g_attn_out.reshape(1, ATTN_WIDTH))


def _conv_branch_tasks(buf_ref, shift_ref, cw_ref, cb_ref, lng_ref, lnb_ref, act_ref):
    lead = HALO - (CONV_KERNEL - 1)
    groups = CONV_ROWS // SUBLANES

    def fill_shifts(r0):
        for s in range(1, SUBLANES):
            shift_ref[s - 1] = buf_ref[r0 + s:r0 + s + MIXOUT_SUB + SHIFT_SPAN, :]

    def chunk(r0, r):
        y = jnp.broadcast_to(cb_ref[...][None], (groups, SUBLANES, CONV_WIDTH_CH))
        for t in range(CONV_KERNEL):
            s = (lead + t) % SUBLANES
            base = r * CONV_ROWS + lead + t - s
            if s == 0:
                rows = buf_ref[r0 + base:r0 + base + CONV_ROWS, :]
            else:
                rows = shift_ref[s - 1, base:base + CONV_ROWS, :]
            y = y + cw_ref[t][None] * rows.reshape(groups, SUBLANES, CONV_WIDTH_CH)
        y = y.reshape(CONV_ROWS, CONV_WIDTH_CH)
        mu = jnp.mean(y, axis=-1, keepdims=True)
        yc = y - mu
        var = jnp.mean(yc * yc, axis=-1, keepdims=True)
        yn = yc * lax.rsqrt(var + LN_EPS) * lng_ref[...] + lnb_ref[...]
        act_ref[r0 + r * CONV_ROWS:r0 + (r + 1) * CONV_ROWS, :] = (
            yn * _sigmoid(yn)).astype(BF16)

    tasks = []
    for r0 in range(0, MIXOUT_TM, MIXOUT_SUB):
        tasks.append(functools.partial(fill_shifts, r0))
        for r in range(MIXOUT_SUB // CONV_ROWS):
            tasks.append(functools.partial(chunk, r0, r))
    return tasks


def _mixout_kernel(x_ref, mod_ref, a_ref, u_ref, halo_ref, cw_ref, cb_ref, lng_ref,
                   lnb_ref, w_ref, gpost_ref, o_ref, buf_ref, shift_ref, act_ref):
    i = pl.program_id(1)
    has_prev = (i > 0).astype(F32)
    buf_ref[0:HALO, :] = halo_ref[0] * has_prev
    buf_ref[HALO:HALO + MIXOUT_TM, :] = u_ref[0]
    tasks = _conv_branch_tasks(buf_ref, shift_ref, cw_ref, cb_ref, lng_ref, lnb_ref, act_ref)
    n_sub = MIXOUT_TM // MIXOUT_SUB
    per_sub = len(tasks) // n_sub
    gate = mod_ref[0, 5:6, :]
    for s in range(n_sub):
        for task in tasks[s * per_sub:(s + 1) * per_sub]:
            task()
        rows = slice(s * MIXOUT_SUB, (s + 1) * MIXOUT_SUB)
        m = (_dot(a_ref[0, rows, :], w_ref[0:ATTN_WIDTH, :])
             + _dot(act_ref[rows, :], w_ref[ATTN_WIDTH:D_MODEL, :]))
        o_ref[0, rows, :] = x_ref[0, rows, :] + (1.0 + gate) * _rms(m, gpost_ref[...])


def _mix_out(x, mod, a, u, conv_w, conv_b, ln_g, ln_b, w_out_mix, g_post):
    bsz, seq, d = x.shape
    tm = MIXOUT_TM
    cw = CONV_WIDTH_CH
    per = tm // HALO
    return pl.pallas_call(
        _mixout_kernel,
        grid=(bsz, seq // tm),
        in_specs=[
            pl.BlockSpec((1, tm, d), lambda b, i: (b, i, 0)),
            pl.BlockSpec((1, 3 * N_SUBLAYERS, d), lambda b, i: (b, 0, 0)),
            pl.BlockSpec((1, tm, ATTN_WIDTH), lambda b, i: (b, i, 0)),
            pl.BlockSpec((1, tm, cw), lambda b, i: (b, i, 0)),
            pl.BlockSpec((1, HALO, cw), lambda b, i: (b, jnp.maximum(i * per - 1, 0), 0)),
            _const_spec((CONV_KERNEL, SUBLANES, cw)),
            _const_spec((SUBLANES, cw)),
            _const_spec((1, cw)),
            _const_spec((1, cw)),
            _const_spec((d, d)),
            _const_spec((1, d)),
        ],
        out_specs=pl.BlockSpec((1, tm, d), lambda b, i: (b, i, 0)),
        out_shape=jax.ShapeDtypeStruct(x.shape, F32),
        scratch_shapes=[
            pltpu.VMEM((HALO + tm, cw), F32),
            pltpu.VMEM((SUBLANES - 1, MIXOUT_SUB + SHIFT_SPAN, cw), F32),
            pltpu.VMEM((tm, cw), BF16),
        ],
        compiler_params=pltpu.CompilerParams(vmem_limit_bytes=VMEM_LIMIT),
        name="mix_out",
    )(x, mod, a, u, u,
      jnp.broadcast_to(conv_w[:, None, :], (CONV_KERNEL, SUBLANES, cw)),
      jnp.broadcast_to(conv_b[None, :], (SUBLANES, cw)), ln_g.reshape(1, cw),
      ln_b.reshape(1, cw), w_out_mix, g_post.reshape(1, d))


def kernel(x, c, w_ada, b_ada, g_pre_ff1, g_post_ff1, ff1_w_in, ff1_w_out, g_pre_mix, g_post_mix, w_in_mix, g_attn_out, conv_w, conv_b, conv_ln_g, conv_ln_b, w_out_mix, g_pre_ff2, g_post_ff2, ff2_w_in, ff2_w_out):
    mod = _adaln(c, w_ada, b_ada)
    h, (w_in_mix_b, w_out_mix_b, ff2_w_in_b, ff2_w_out_b) = _ffn(
        x, mod, g_pre_ff1, g_post_ff1, ff1_w_in.astype(BF16), ff1_w_out.astype(BF16),
        0, 0.5, cast_along=(w_in_mix, w_out_mix, ff2_w_in, ff2_w_out))
    qkv, u = _mix_in(h, mod, g_pre_mix, w_in_mix_b)
    a = _attention(qkv, g_attn_out)
    h = _mix_out(h, mod, a, u, conv_w, conv_b, conv_ln_g, conv_ln_b, w_out_mix_b,
                 g_post_mix)
    h, _ = _ffn(h, mod, g_pre_ff2, g_post_ff2, ff2_w_in_b, ff2_w_out_b, 2, 0.5)
    return h
```

```python
import functools

import jax
import jax.numpy as jnp
from jax import lax
from jax.experimental import pallas as pl
from jax.experimental.pallas import tpu as pltpu

D_MODEL = 1024
ATTN_WIDTH = D_MODEL // 2
HEAD_DIM = 64
N_HEADS = ATTN_WIDTH // HEAD_DIM
CONV_WIDTH_CH = D_MODEL - ATTN_WIDTH
CONV_KERNEL = 31
D_FF = 2816
N_SUBLAYERS = 3
MIX_IN = 3 * ATTN_WIDTH + 2 * CONV_WIDTH_CH
RMS_EPS = 1e-6
LN_EPS = 1e-5

LANES = 128
SUBLANES = 8
MXU_DIM = 256
VMEM_LIMIT = 56 * 1024 * 1024

FFN_TM = 1024
FFN_SUB = 256
FFN_CHUNK = D_FF
MIX_TM = 1024
MIX_SUB = 256
MIXOUT_TM = 512
MIXOUT_SUB = 256
CONV_ROWS = 32
HALO = 32
SHIFT_SPAN = HALO - SUBLANES
ATT_Q = 64
ATT_BLOCKS = 8
ATT_PAIRS = ATTN_WIDTH // LANES
ATT_K0 = 256
ATT_BACK = ATT_K0 - ATT_Q
ATT_KB = ATT_Q
LOGW_FLOOR = -88.0
LOG2E = 1.4426950408889634

F32 = jnp.float32
BF16 = jnp.bfloat16


def _sigmoid(x):
    return 1.0 / (1.0 + jnp.exp(-x))


def _dot(a, b):
    return jnp.dot(a, b, preferred_element_type=F32)


def _rms(x, g):
    ms = jnp.mean(x * x, axis=-1, keepdims=True)
    return x * lax.rsqrt(ms + RMS_EPS) * g


def _adaln_kernel(c_ref, w_ref, b_ref, o_ref):
    c = c_ref[...]
    s = c * _sigmoid(c)
    o_ref[...] = _dot(s.astype(BF16), w_ref[...].astype(BF16)) + b_ref[...]


def _adaln(c, w_ada, b_ada):
    bsz = c.shape[0]
    n = w_ada.shape[1]
    tn = 1024
    c_pad = jnp.zeros((SUBLANES, D_MODEL), F32).at[:bsz].set(c)
    out = pl.pallas_call(
        _adaln_kernel,
        grid=(n // tn,),
        in_specs=[
            pl.BlockSpec((SUBLANES, D_MODEL), lambda j: (0, 0)),
            pl.BlockSpec((D_MODEL, tn), lambda j: (0, j)),
            pl.BlockSpec((1, tn), lambda j: (0, j)),
        ],
        out_specs=pl.BlockSpec((SUBLANES, tn), lambda j: (0, j)),
        out_shape=jax.ShapeDtypeStruct((SUBLANES, n), F32),
        name="adaln",
    )(c_pad, w_ada, b_ada.reshape(1, n))
    return out[:bsz].reshape(bsz, 3 * N_SUBLAYERS, D_MODEL)


def _modulated(x, mod_ref, sub, g_pre):
    shift = mod_ref[0, 3 * sub:3 * sub + 1, :]
    scale = mod_ref[0, 3 * sub + 1:3 * sub + 2, :]
    return _rms(x, g_pre) * (1.0 + scale) + shift


def _ffn_rows(x, mod_ref, gpre_ref, gpost_ref, win_ref, wout_ref, sub, res_w):
    hb = _modulated(x, mod_ref, sub, gpre_ref[...]).astype(BF16)
    f = None
    for c in range(D_FF // FFN_CHUNK):
        lo = c * FFN_CHUNK
        g = _dot(hb, win_ref[:, lo:lo + FFN_CHUNK])
        u = _dot(hb, win_ref[:, D_FF + lo:D_FF + lo + FFN_CHUNK])
        act = (g * _sigmoid(g) * u).astype(BF16)
        part = _dot(act, wout_ref[lo:lo + FFN_CHUNK, :])
        f = part if f is None else f + part
    y = _rms(f, gpost_ref[...])
    gate = mod_ref[0, 3 * sub + 2:3 * sub + 3, :]
    return x + res_w * (1.0 + gate) * y


def _ffn_kernel(x_ref, mod_ref, gpre_ref, gpost_ref, win_ref, wout_ref, *rest,
                sub, res_w):
    n_cast = len(rest) // 2
    o_ref = rest[n_cast]
    for s in range(FFN_TM // FFN_SUB):
        rows = slice(s * FFN_SUB, (s + 1) * FFN_SUB)
        o_ref[0, rows, :] = _ffn_rows(x_ref[0, rows, :], mod_ref, gpre_ref, gpost_ref,
                                      win_ref, wout_ref, sub, res_w)
    for src_ref, dst_ref in zip(rest[:n_cast], rest[n_cast + 1:]):
        dst_ref[...] = src_ref[...].astype(BF16)


def _const_spec(shape):
    nd = len(shape)
    return pl.BlockSpec(shape, lambda *_: (0,) * nd, pipeline_mode=pl.Buffered(1))


def _row_blocks(rows, steps):
    bf16_rows = 2 * SUBLANES
    return max(n for n in range(1, steps + 1)
               if rows % n == 0 and (rows // n) % bf16_rows == 0)


def _ffn(x, mod, g_pre, g_post, w_in, w_out, sub, res_w, cast_along=()):
    bsz, seq, d = x.shape
    tm = FFN_TM
    n_i = seq // tm
    cast_specs, cast_shapes = [], []
    for w in cast_along:
        n_blk = _row_blocks(w.shape[0], bsz * n_i)
        spec = pl.BlockSpec(
            (w.shape[0] // n_blk, w.shape[1]),
            lambda b, i, n_blk=n_blk: (jnp.minimum(b * n_i + i, n_blk - 1), 0))
        cast_specs.append(spec)
        cast_shapes.append(jax.ShapeDtypeStruct(w.shape, BF16))
    out = pl.pallas_call(
        functools.partial(_ffn_kernel, sub=sub, res_w=res_w),
        grid=(bsz, n_i),
        in_specs=[
            pl.BlockSpec((1, tm, d), lambda b, i: (b, i, 0)),
            pl.BlockSpec((1, 3 * N_SUBLAYERS, d), lambda b, i: (b, 0, 0)),
            _const_spec((1, d)),
            _const_spec((1, d)),
            _const_spec((d, 2 * D_FF)),
            _const_spec((D_FF, d)),
        ] + cast_specs,
        out_specs=[pl.BlockSpec((1, tm, d), lambda b, i: (b, i, 0))] + cast_specs,
        out_shape=[jax.ShapeDtypeStruct(x.shape, F32)] + cast_shapes,
        compiler_params=pltpu.CompilerParams(
            dimension_semantics=("arbitrary", "arbitrary"),
            vmem_limit_bytes=VMEM_LIMIT),
        name=f"ffn{sub}",
    )(x, mod, g_pre.reshape(1, d), g_post.reshape(1, d), w_in, w_out, *cast_along)
    return out[0], out[1:]


def _mixin_kernel(x_ref, mod_ref, gpre_ref, w_ref, qkv_ref, u_ref):
    aw = ATTN_WIDTH
    for s in range(MIX_TM // MIX_SUB):
        rows = slice(s * MIX_SUB, (s + 1) * MIX_SUB)
        hb = _modulated(x_ref[0, rows, :], mod_ref, 1, gpre_ref[...]).astype(BF16)
        q = _dot(hb, w_ref[:, 0:aw]) * (HEAD_DIM ** -0.5)
        qkv_ref[0, rows, 0:aw] = q.astype(BF16)
        kv = _dot(hb, w_ref[:, aw:3 * aw])
        qkv_ref[0, rows, aw:3 * aw] = kv.astype(BF16)
        cv = _dot(hb, w_ref[:, 3 * aw:3 * aw + CONV_WIDTH_CH])
        cg = _dot(hb, w_ref[:, 3 * aw + CONV_WIDTH_CH:MIX_IN])
        u_ref[0, rows, :] = cv * _sigmoid(cg)


def _mix_in(x, mod, g_pre, w_in_mix):
    bsz, seq, d = x.shape
    tm = MIX_TM
    return pl.pallas_call(
        _mixin_kernel,
        grid=(bsz, seq // tm),
        in_specs=[
            pl.BlockSpec((1, tm, d), lambda b, i: (b, i, 0)),
            pl.BlockSpec((1, 3 * N_SUBLAYERS, d), lambda b, i: (b, 0, 0)),
            _const_spec((1, d)),
            _const_spec((d, MIX_IN)),
        ],
        out_specs=[
            pl.BlockSpec((1, tm, 3 * ATTN_WIDTH), lambda b, i: (b, i, 0)),
            pl.BlockSpec((1, tm, CONV_WIDTH_CH), lambda b, i: (b, i, 0)),
        ],
        out_shape=[
            jax.ShapeDtypeStruct((bsz, seq, 3 * ATTN_WIDTH), BF16),
            jax.ShapeDtypeStruct((bsz, seq, CONV_WIDTH_CH), F32),
        ],
        compiler_params=pltpu.CompilerParams(vmem_limit_bytes=VMEM_LIMIT),
        name="mix_in",
    )(x, mod, g_pre.reshape(1, d), w_in_mix)


def _softplus(z):
    t = jnp.exp2(jnp.abs(z) * (-LOG2E))
    return jnp.maximum(z, 0.0) + jnp.log(1.0 + t)


def _qk(qm, k):
    return lax.dot_general(qm, k, (((1,), (1,)), ((), ())),
                           preferred_element_type=F32)


def _attn_kernel(q_ref, k_ref, v_ref, g_ref, o_ref, tri_ref, z_ref, sp_ref, after_ref):
    b = pl.program_id(0)
    step = pl.program_id(1)
    back_chunks = ATT_BACK // ATT_Q

    @pl.when((b == 0) & (step == 0))
    def _():
        r = lax.broadcasted_iota(jnp.int32, (ATT_K0, ATT_K0), 0)
        c = lax.broadcasted_iota(jnp.int32, (ATT_K0, ATT_K0), 1)
        for missing in range(back_chunks + 1):
            dead = (r >= ATT_BACK - missing * ATT_Q) & (r < ATT_BACK)
            tri_ref[missing] = jnp.where((r > c) & jnp.logical_not(dead), -1.0, 0.0).astype(BF16)

    first = lax.broadcasted_iota(jnp.int32, (1, LANES), 1) < HEAD_DIM
    rows2 = 2 * ATT_Q
    row = lax.broadcasted_iota(jnp.int32, (rows2, LANES), 0)
    col = lax.broadcasted_iota(jnp.int32, (rows2, LANES), 1) + (ATT_K0 - LANES)
    causal = (col - ATT_BACK) < jnp.bitwise_and(row, ATT_Q - 1)
    tri_b = tri_ref[0, 0:ATT_KB, 0:ATT_KB]

    def stacked_q(j, p):
        qp = q_ref[0, j * ATT_Q:(j + 1) * ATT_Q, p * LANES:(p + 1) * LANES]
        zero = jnp.zeros_like(qp)
        return jnp.concatenate(
            [jnp.where(first, qp, zero), jnp.where(first, zero, qp)], axis=0)

    def weighted_values(w, v):
        wb = w.astype(BF16)
        return jnp.where(first, _dot(wb[0:ATT_Q], v), _dot(wb[ATT_Q:rows2], v))

    def unfinished(carries):
        worst = functools.reduce(jnp.maximum, carries)
        return (jnp.max(worst) > LOGW_FLOOR).astype(jnp.int32)

    def write_normalized(j, accs):
        for p in range(ATT_PAIRS):
            lanes = slice(p * LANES, (p + 1) * LANES)
            o = accs[p]
            sq = o * o
            s_all = jnp.sum(sq, axis=-1, keepdims=True)
            s_first = jnp.sum(jnp.where(first, sq, 0.0), axis=-1, keepdims=True)
            ms = jnp.where(first, s_first, s_all - s_first) * (1.0 / HEAD_DIM)
            o_ref[0, j * ATT_Q:(j + 1) * ATT_Q, lanes] = (
                o * lax.rsqrt(ms + RMS_EPS) * g_ref[:, lanes]).astype(BF16)

    blks = [step * ATT_BLOCKS + j for j in range(ATT_BLOCKS)]
    back_rows = [pl.multiple_of(jnp.maximum(blk * ATT_Q - ATT_BACK, 0), ATT_Q) for blk in blks]
    diag_rows = [pl.multiple_of(blk * ATT_Q, ATT_Q) for blk in blks]

    def window_keys(j, lanes):
        return jnp.concatenate([k_ref[0, pl.ds(back_rows[j], ATT_BACK), lanes],
                                k_ref[0, pl.ds(diag_rows[j], ATT_Q), lanes]], axis=0)

    def window_values(j, lanes):
        chunks = []
        for i in range(back_chunks):
            rows = pl.multiple_of(back_rows[j] + i * ATT_Q, ATT_Q)
            chunk = v_ref[0, pl.ds(rows, ATT_Q), lanes]
            chunks.append(jnp.where(blks[j] > i, chunk, jnp.zeros_like(chunk)))
        chunks.append(v_ref[0, pl.ds(diag_rows[j], ATT_Q), lanes])
        return jnp.concatenate(chunks, axis=0)

    def window_softplus(z):
        sp = _softplus(z)
        return sp[:, 0:LANES], jnp.where(causal, sp[:, LANES:ATT_K0], 0.0), sp

    for j in range(ATT_BLOCKS):
        for p in range(ATT_PAIRS):
            lanes = slice(p * LANES, (p + 1) * LANES)
            z_ref[j * ATT_PAIRS + p] = _qk(stacked_q(j, p), window_keys(j, lanes))
    least = [[] for _ in range(ATT_BLOCKS)]
    for j in range(ATT_BLOCKS):
        tri = tri_ref[jnp.maximum(back_chunks - blks[j], 0)]
        for p in range(ATT_PAIRS):
            slot = j * ATT_PAIRS + p
            z = z_ref[slot]
            sp_lo, sp_hi, sp = window_softplus(z)
            z_ref[slot] = z - sp
            least[j].append(jnp.min(jnp.sum(sp_lo + sp_hi, axis=-1, keepdims=True)))
            sp_ref[slot, :, 0:LANES] = sp_lo.astype(BF16)
            sp_ref[slot, :, LANES:ATT_K0] = sp_hi.astype(BF16)
            after_ref[slot] = _dot(sp_ref[slot], tri)
    blocks = []
    for j in range(ATT_BLOCKS):
        accs = []
        for p in range(ATT_PAIRS):
            slot = j * ATT_PAIRS + p
            lanes = slice(p * LANES, (p + 1) * LANES)
            w = jnp.exp(z_ref[slot] + after_ref[slot])
            w = jnp.concatenate(
                [w[:, 0:LANES], jnp.where(causal, w[:, LANES:ATT_K0], 0.0)], axis=1)
            accs.append(weighted_values(w, window_values(j, lanes)))
        write_normalized(j, accs)
        blocks.append((blks[j], functools.reduce(jnp.minimum, least[j]) < -LOGW_FLOOR, accs))

    for j, (blk, more, accs) in enumerate(blocks):
        @pl.when(jnp.logical_and(blk > back_chunks, more))
        def _(j=j, blk=blk, accs=accs):
            carries = []
            for p in range(ATT_PAIRS):
                lanes = slice(p * LANES, (p + 1) * LANES)
                z = _qk(stacked_q(j, p), window_keys(j, lanes))
                sp_lo, sp_hi, _ = window_softplus(z)
                carries.append(-jnp.sum(sp_lo + sp_hi, axis=-1, keepdims=True))

            def cond(st):
                jb, more, _, _ = st
                return jnp.logical_and(jb >= 0, more > 0)

            def body(st):
                jb, _, carries, accs = st
                ks = pl.multiple_of(jb * ATT_KB, ATT_KB)
                new_c, new_a = [], []
                for p in range(ATT_PAIRS):
                    lanes = slice(p * LANES, (p + 1) * LANES)
                    kb = k_ref[0, pl.ds(ks, ATT_KB), lanes]
                    vb = v_ref[0, pl.ds(ks, ATT_KB), lanes]
                    z = _qk(stacked_q(j, p), kb)
                    sp = _softplus(z)
                    w = jnp.exp((z - sp) + _dot(sp.astype(BF16), tri_b) + carries[p])
                    new_a.append(accs[p] + weighted_values(w, vb))
                    new_c.append(carries[p] - jnp.sum(sp, axis=-1, keepdims=True))
                return jb - 1, unfinished(new_c), tuple(new_c), tuple(new_a)

            init = (blk - back_chunks - 1, jnp.int32(1), tuple(carries), tuple(accs))
            write_normalized(j, lax.while_loop(cond, body, init)[3])


def _attention(qkv, g_attn_out):
    bsz, seq, _ = qkv.shape
    tq = ATT_BLOCKS * ATT_Q
    slots = ATT_BLOCKS * ATT_PAIRS
    kv_spec = functools.partial(pl.BlockSpec, (1, seq, ATTN_WIDTH))
    return pl.pallas_call(
        _attn_kernel,
        grid=(bsz, seq // tq),
        in_specs=[
            pl.BlockSpec((1, tq, ATTN_WIDTH), lambda b, i: (b, i, 0)),
            kv_spec(lambda b, i: (b, 0, 1)),
            kv_spec(lambda b, i: (b, 0, 2)),
            _const_spec((1, ATTN_WIDTH)),
        ],
        out_specs=pl.BlockSpec((1, tq, ATTN_WIDTH), lambda b, i: (b, i, 0)),
        out_shape=jax.ShapeDtypeStruct((bsz, seq, ATTN_WIDTH), BF16),
        scratch_shapes=[
            pltpu.VMEM((ATT_BACK // ATT_Q + 1, ATT_K0, ATT_K0), BF16),
            pltpu.VMEM((slots, 2 * ATT_Q, ATT_K0), F32),
            pltpu.VMEM((slots, 2 * ATT_Q, ATT_K0), BF16),
            pltpu.VMEM((slots, 2 * ATT_Q, ATT_K0), F32),
        ],
        compiler_params=pltpu.CompilerParams(
            dimension_semantics=("arbitrary", "arbitrary"),
            vmem_limit_bytes=VMEM_LIMIT),
        name="attn",
    )(qkv, qkv, qkv, g_attn_out.reshape(1, ATTN_WIDTH))


def _conv_branch_tasks(buf_ref, shift_ref, cw_ref, cb_ref, lng_ref, lnb_ref, act_ref):
    lead = HALO - (CONV_KERNEL - 1)
    groups = CONV_ROWS // SUBLANES

    def fill_shifts(r0):
        for s in range(1, SUBLANES):
            shift_ref[s - 1] = buf_ref[r0 + s:r0 + s + MIXOUT_SUB + SHIFT_SPAN, :]

    def chunk(r0, r):
        y = jnp.broadcast_to(cb_ref[...][None], (groups, SUBLANES, CONV_WIDTH_CH))
        for t in range(CONV_KERNEL):
            s = (lead + t) % SUBLANES
            base = r * CONV_ROWS + lead + t - s
            if s == 0:
                rows = buf_ref[r0 + base:r0 + base + CONV_ROWS, :]
            else:
                rows = shift_ref[s - 1, base:base + CONV_ROWS, :]
            y = y + cw_ref[t][None] * rows.reshape(groups, SUBLANES, CONV_WIDTH_CH)
        y = y.reshape(CONV_ROWS, CONV_WIDTH_CH)
        mu = jnp.mean(y, axis=-1, keepdims=True)
        yc = y - mu
        var = jnp.mean(yc * yc, axis=-1, keepdims=True)
        yn = yc * lax.rsqrt(var + LN_EPS) * lng_ref[...] + lnb_ref[...]
        act_ref[r0 + r * CONV_ROWS:r0 + (r + 1) * CONV_ROWS, :] = (
            yn * _sigmoid(yn)).astype(BF16)

    tasks = []
    for r0 in range(0, MIXOUT_TM, MIXOUT_SUB):
        tasks.append(functools.partial(fill_shifts, r0))
        for r in range(MIXOUT_SUB // CONV_ROWS):
            tasks.append(functools.partial(chunk, r0, r))
    return tasks


def _mixout_kernel(x_ref, mod_ref, a_ref, u_ref, halo_ref, cw_ref, cb_ref, lng_ref,
                   lnb_ref, w_ref, gpost_ref, o_ref, buf_ref, shift_ref, act_ref):
    i = pl.program_id(1)
    has_prev = (i > 0).astype(F32)
    buf_ref[0:HALO, :] = halo_ref[0] * has_prev
    buf_ref[HALO:HALO + MIXOUT_TM, :] = u_ref[0]
    tasks = _conv_branch_tasks(buf_ref, shift_ref, cw_ref, cb_ref, lng_ref, lnb_ref, act_ref)
    n_sub = MIXOUT_TM // MIXOUT_SUB
    per_sub = len(tasks) // n_sub
    gate = mod_ref[0, 5:6, :]
    for s in range(n_sub):
        for task in tasks[s * per_sub:(s + 1) * per_sub]:
            task()
        rows = slice(s * MIXOUT_SUB, (s + 1) * MIXOUT_SUB)
        mixed = jnp.concatenate([a_ref[0, rows, :], act_ref[rows, :]], axis=-1)
        m = _dot(mixed, w_ref[...])
        o_ref[0, rows, :] = x_ref[0, rows, :] + (1.0 + gate) * _rms(m, gpost_ref[...])


def _mix_out(x, mod, a, u, conv_w, conv_b, ln_g, ln_b, w_out_mix, g_post):
    bsz, seq, d = x.shape
    tm = MIXOUT_TM
    cw = CONV_WIDTH_CH
    per = tm // HALO
    return pl.pallas_call(
        _mixout_kernel,
        grid=(bsz, seq // tm),
        in_specs=[
            pl.BlockSpec((1, tm, d), lambda b, i: (b, i, 0)),
            pl.BlockSpec((1, 3 * N_SUBLAYERS, d), lambda b, i: (b, 0, 0)),
            pl.BlockSpec((1, tm, ATTN_WIDTH), lambda b, i: (b, i, 0)),
            pl.BlockSpec((1, tm, cw), lambda b, i: (b, i, 0)),
            pl.BlockSpec((1, HALO, cw), lambda b, i: (b, jnp.maximum(i * per - 1, 0), 0)),
            _const_spec((CONV_KERNEL, SUBLANES, cw)),
            _const_spec((SUBLANES, cw)),
            _const_spec((1, cw)),
            _const_spec((1, cw)),
            _const_spec((d, d)),
            _const_spec((1, d)),
        ],
        out_specs=pl.BlockSpec((1, tm, d), lambda b, i: (b, i, 0)),
        out_shape=jax.ShapeDtypeStruct(x.shape, F32),
        scratch_shapes=[
            pltpu.VMEM((HALO + tm, cw), F32),
            pltpu.VMEM((SUBLANES - 1, MIXOUT_SUB + SHIFT_SPAN, cw), F32),
            pltpu.VMEM((tm, cw), BF16),
        ],
        compiler_params=pltpu.CompilerParams(vmem_limit_bytes=VMEM_LIMIT),
        name="mix_out",
    )(x, mod, a, u, u,
      jnp.broadcast_to(conv_w[:, None, :], (CONV_KERNEL, SUBLANES, cw)),
      jnp.broadcast_to(conv_b[None, :], (SUBLANES, cw)), ln_g.reshape(1, cw),
      ln_b.reshape(1, cw), w_out_mix, g_post.reshape(1, d))


def kernel(x, c, w_ada, b_ada, g_pre_ff1, g_post_ff1, ff1_w_in, ff1_w_out, g_pre_mix, g_post_mix, w_in_mix, g_attn_out, conv_w, conv_b, conv_ln_g, conv_ln_b, w_out_mix, g_pre_ff2, g_post_ff2, ff2_w_in, ff2_w_out):
    mod = _adaln(c, w_ada, b_ada)
    h, (w_in_mix_b, w_out_mix_b, ff2_w_in_b, ff2_w_out_b) = _ffn(
        x, mod, g_pre_ff1, g_post_ff1, ff1_w_in.astype(BF16), ff1_w_out.astype(BF16),
        0, 0.5, cast_along=(w_in_mix, w_out_mix, ff2_w_in, ff2_w_out))
    qkv, u = _mix_in(h, mod, g_pre_mix, w_in_mix_b)
    a = _attention(qkv, g_attn_out)
    h = _mix_out(h, mod, a, u, conv_w, conv_b, conv_ln_g, conv_ln_b, w_out_mix_b,
                 g_post_mix)
    h, _ = _ffn(h, mod, g_pre_ff2, g_post_ff2, ff2_w_in_b, ff2_w_out_b, 2, 0.5)
    return h
```
